```python
import math
import jax, jax.numpy as jnp
from jax import lax
import numpy as np

D_MODEL = 1024
BATCH = 8
SEQ = 4096
DEPTH = 1

DA_HEADS = 8
DA_HEAD_DIM = 64
DA_V_DIM = 2 * DA_HEAD_DIM
DA_ROT_DIM = DA_HEAD_DIM // 4
MLA_HEADS = 8
MLA_Q_LORA = 384
MLA_KV_LORA = 256
MLA_NOPE = 128
MLA_ROPE = 64
MLA_V = 128
ROPE_THETA = 500000.0
N_GROUPS = 4
EXPERTS_PER_GROUP = 8
N_EXPERTS = N_GROUPS * EXPERTS_PER_GROUP
TOP_K = 2
D_EXPERT = 512
MOE_BLOCK = 128
Q_BLOCK = 128
NORM_EPS = 1e-6

DA_Q_COLS = DA_HEADS * 2 * DA_HEAD_DIM
DA_K_COLS = DA_HEADS * 2 * DA_HEAD_DIM
DA_V_COLS = DA_HEADS * DA_V_DIM
GATE_COLS = 2 * D_MODEL
IN_COLS = DA_Q_COLS + DA_K_COLS + DA_V_COLS + MLA_Q_LORA + MLA_KV_LORA + MLA_ROPE + GATE_COLS
SPLIT_1 = DA_Q_COLS
SPLIT_2 = SPLIT_1 + DA_K_COLS
SPLIT_3 = SPLIT_2 + DA_V_COLS
SPLIT_4 = SPLIT_3 + MLA_Q_LORA
SPLIT_5 = SPLIT_4 + MLA_KV_LORA
SPLIT_6 = SPLIT_5 + MLA_ROPE

kernel_name = 'hybrid_diffattn_mla_hmoe_block'


def rms_norm(x, g):
    xf = x.astype(jnp.float32)
    y = xf * lax.rsqrt(jnp.mean(xf * xf, axis=-1, keepdims=True) + NORM_EPS)
    return (y * g.astype(jnp.float32)).astype(x.dtype)


def rope_tables(positions, dim):
    inv = ROPE_THETA ** (-jnp.arange(0, dim, 2, dtype=jnp.float32) / dim)
    ang = positions.astype(jnp.float32)[..., None] * inv
    return jnp.cos(ang), jnp.sin(ang)


def apply_rope(x, cos, sin):
    x1, x2 = jnp.split(x, 2, axis=-1)
    c = cos[:, :, None, :]
    s = sin[:, :, None, :]
    return jnp.concatenate([x1 * c - x2 * s, x2 * c + x1 * s], axis=-1).astype(x.dtype)


def partial_rope(x, cos, sin, rot_dim):
    return jnp.concatenate([apply_rope(x[..., :rot_dim], cos, sin), x[..., rot_dim:]], axis=-1)


def causal_block_attention(q_maps, k_maps, coefs, v, scale):
    S = v.shape[1]
    n_blocks = S // Q_BLOCK
    outs = []
    for i in range(n_blocks):
        q0, q1 = i * Q_BLOCK, (i + 1) * Q_BLOCK
        mask = jnp.arange(q1)[None, :] <= jnp.arange(q0, q1)[:, None]
        w = None
        for q, k, c in zip(q_maps, k_maps, coefs):
            s = jnp.einsum('bqhd,bkhd->bhqk', q[:, q0:q1], k[:, :q1],
                           preferred_element_type=jnp.float32) * scale
            p = jax.nn.softmax(jnp.where(mask, s, -jnp.inf), axis=-1) * c
            w = p if w is None else w + p
        outs.append(jnp.einsum('bhqk,bkhd->bqhd', w.astype(v.dtype), v[:, :q1]))
    return jnp.concatenate(outs, axis=1)


def diff_attention(q, k, v, cos, sin, q_g, k_g, lq1, lk1, lq2, lk2, subln_g, lambda_init):
    B, S = q.shape[:2]
    q = partial_rope(rms_norm(q.reshape(B, S, 2 * DA_HEADS, DA_HEAD_DIM), q_g), cos, sin, DA_ROT_DIM)
    k = partial_rope(rms_norm(k.reshape(B, S, 2 * DA_HEADS, DA_HEAD_DIM), k_g), cos, sin, DA_ROT_DIM)
    v = v.reshape(B, S, DA_HEADS, DA_V_DIM)
    f32 = jnp.float32
    lam = (jnp.exp(jnp.sum(lq1.astype(f32) * lk1.astype(f32)))
           - jnp.exp(jnp.sum(lq2.astype(f32) * lk2.astype(f32))) + lambda_init)
    o = causal_block_attention([q[:, :, 0::2], q[:, :, 1::2]], [k[:, :, 0::2], k[:, :, 1::2]],
                               [1.0, -lam], v, DA_HEAD_DIM ** -0.5)
    o = rms_norm(o, subln_g) * (1.0 - lambda_init)
    return o.reshape(B, S, DA_HEADS * DA_V_DIM)


def mla_attention(c_q, c_kv, k_rope, cos, sin, q_lora_g, w_uq, kv_lora_g, w_ukv, q_g, k_nope_g, k_rope_g):
    B, S = c_q.shape[:2]
    q = (rms_norm(c_q, q_lora_g) @ w_uq).reshape(B, S, MLA_HEADS, MLA_NOPE + MLA_ROPE)
    kv = (rms_norm(c_kv, kv_lora_g) @ w_ukv).reshape(B, S, MLA_HEADS, MLA_NOPE + MLA_V)
    k_nope, v = kv[..., :MLA_NOPE], kv[..., MLA_NOPE:]
    q = rms_norm(q, q_g)
    q = jnp.concatenate([q[..., :MLA_NOPE], apply_rope(q[..., MLA_NOPE:], cos, sin)], axis=-1)
    k_nope = rms_norm(k_nope, k_nope_g)
    k_r = apply_rope(rms_norm(k_rope, k_rope_g)[:, :, None, :], cos, sin)
    k = jnp.concatenate([k_nope, jnp.broadcast_to(k_r, (B, S, MLA_HEADS, MLA_ROPE))], axis=-1)
    o = causal_block_attention([q], [k], [1.0], v, (MLA_NOPE + MLA_ROPE) ** -0.5)
    return o.reshape(B, S, MLA_HEADS * MLA_V)


def hier_moe(h, w_group, b_group, w_router, b_router, w1, w3, w2):
    B, S, D = h.shape
    T = B * S
    hf = h.reshape(T, D)
    f32 = jnp.float32
    glog = (hf @ w_group).astype(f32) + b_group.astype(f32)
    g_sel = jnp.argmax(glog, axis=-1)
    p_g = jnp.take_along_axis(jax.nn.softmax(glog, axis=-1), g_sel[:, None], axis=-1)
    elog = ((hf @ w_router).astype(f32) + b_router.astype(f32)).reshape(T, N_GROUPS, EXPERTS_PER_GROUP)
    elog = jnp.take_along_axis(elog, g_sel[:, None, None], axis=1)[:, 0]
    top_v, top_i = lax.top_k(elog, TOP_K)
    weight = p_g * jax.nn.softmax(top_v, axis=-1)
    expert = g_sel[:, None] * EXPERTS_PER_GROUP + top_i
    TK = T * TOP_K
    flat_e = expert.reshape(TK)
    flat_tok = jnp.repeat(jnp.arange(T, dtype=jnp.int32), TOP_K)
    order = jnp.argsort(flat_e)
    se, stok, sw = flat_e[order], flat_tok[order], weight.reshape(TK)[order]
    counts = jnp.bincount(flat_e, length=N_EXPERTS)
    starts = jnp.cumsum(counts) - counts
    padded = (counts + MOE_BLOCK - 1) // MOE_BLOCK * MOE_BLOCK
    pends = jnp.cumsum(padded)
    pstarts = pends - padded
    dest = pstarts[se] + (jnp.arange(TK) - starts[se])
    n_chunks = -(-TK // MOE_BLOCK) + N_EXPERTS
    P = n_chunks * MOE_BLOCK
    row_tok = jnp.full((P,), T, dtype=jnp.int32).at[dest].set(stok)
    x_pad = jnp.concatenate([hf, jnp.zeros((1, D), hf.dtype)], axis=0)
    xb = x_pad[row_tok].reshape(n_chunks, MOE_BLOCK, D)
    chunk_e = jnp.minimum(jnp.searchsorted(pends, jnp.arange(n_chunks) * MOE_BLOCK, side='right'),
                          N_EXPERTS - 1)

    def expert_ffn(args):
        xc, e = args
        return (jax.nn.silu(xc @ w1[e]) * (xc @ w3[e])) @ w2[e]

    yb = lax.map(expert_ffn, (xb, chunk_e)).reshape(P, D)
    y = (yb[dest].astype(f32) * sw[:, None]).astype(h.dtype)
    out = jax.ops.segment_sum(y, stok, num_segments=T)
    return out.reshape(B, S, D)


def setup_inputs(seed: int = 0) -> dict:
    key = jax.random.key(seed)
    ks = jax.random.split(key, 32)
    L, D = DEPTH, D_MODEL
    f32 = jnp.float32

    def nrm(k, shape, scale):
        return jax.random.normal(k, shape, f32) * scale

    def gain(k, shape):
        return 1.0 + 0.02 * jax.random.normal(k, shape, f32)

    x = jax.random.normal(ks[0], (BATCH, SEQ, D), f32)
    offsets = jax.random.randint(ks[1], (BATCH, 1), 0, 2048, dtype=jnp.int32)
    positions = offsets + jnp.arange(SEQ, dtype=jnp.int32)[None, :]
    return {
        'x': x,
        'positions': positions,
        'attn_norm_g': gain(ks[2], (L, D)),
        'w_in': nrm(ks[3], (L, D, IN_COLS), D ** -0.5),
        'b_gate': nrm(ks[4], (L, 2, D), 0.02),
        'da_q_norm_g': gain(ks[5], (L, DA_HEAD_DIM)),
        'da_k_norm_g': gain(ks[6], (L, DA_HEAD_DIM)),
        'da_lambda_q1': nrm(ks[7], (L, DA_HEAD_DIM), 0.1),
        'da_lambda_k1': nrm(ks[8], (L, DA_HEAD_DIM), 0.1),
        'da_lambda_q2': nrm(ks[9], (L, DA_HEAD_DIM), 0.1),
        'da_lambda_k2': nrm(ks[10], (L, DA_HEAD_DIM), 0.1),
        'da_subln_g': gain(ks[11], (L, DA_V_DIM)),
        'mla_q_lora_g': gain(ks[12], (L, MLA_Q_LORA)),
        'mla_w_uq': nrm(ks[13], (L, MLA_Q_LORA, MLA_HEADS * (MLA_NOPE + MLA_ROPE)), MLA_Q_LORA ** -0.5),
        'mla_kv_lora_g': gain(ks[14], (L, MLA_KV_LORA)),
        'mla_w_ukv': nrm(ks[15], (L, MLA_KV_LORA, MLA_HEADS * (MLA_NOPE + MLA_V)), MLA_KV_LORA ** -0.5),
        'mla_q_norm_g': gain(ks[16], (L, MLA_NOPE + MLA_ROPE)),
        'mla_k_nope_norm_g': gain(ks[17], (L, MLA_NOPE)),
        'mla_k_rope_norm_g': gain(ks[18], (L, MLA_ROPE)),
        'w_o': nrm(ks[19], (L, D, D), D ** -0.5),
        'ffn_norm_g': gain(ks[20], (L, D)),
        'w_group': nrm(ks[21], (L, D, N_GROUPS), D ** -0.5),
        'b_group': nrm(ks[22], (L, N_GROUPS), 0.01),
        'w_router': nrm(ks[23], (L, D, N_EXPERTS), D ** -0.5),
        'b_router': nrm(ks[24], (L, N_EXPERTS), 0.01),
        'w1': nrm(ks[25], (L, N_EXPERTS, D, D_EXPERT), D ** -0.5),
        'w3': nrm(ks[26], (L, N_EXPERTS, D, D_EXPERT), D ** -0.5),
        'w2': nrm(ks[27], (L, N_EXPERTS, D_EXPERT, D), D_EXPERT ** -0.5),
    }


def reference(x, positions, attn_norm_g, w_in, b_gate, da_q_norm_g, da_k_norm_g,
              da_lambda_q1, da_lambda_k1, da_lambda_q2, da_lambda_k2, da_subln_g,
              mla_q_lora_g, mla_w_uq, mla_kv_lora_g, mla_w_ukv, mla_q_norm_g,
              mla_k_nope_norm_g, mla_k_rope_norm_g, w_o, ffn_norm_g, w_group, b_group,
              w_router, b_router, w1, w3, w2):
    B, S, D = x.shape
    da_cos, da_sin = rope_tables(positions, DA_ROT_DIM)
    mla_cos, mla_sin = rope_tables(positions, MLA_ROPE)
    for l in range(DEPTH):
        lambda_init = 0.8 - 0.6 * math.exp(-0.3 * l)
        h = rms_norm(x, attn_norm_g[l])
        proj = h @ w_in[l]
        da_q, da_k, da_v, c_q, c_kv, k_rope, gate_logits = jnp.split(
            proj, [SPLIT_1, SPLIT_2, SPLIT_3, SPLIT_4, SPLIT_5, SPLIT_6], axis=-1)
        o_da = diff_attention(da_q, da_k, da_v, da_cos, da_sin, da_q_norm_g[l], da_k_norm_g[l],
                              da_lambda_q1[l], da_lambda_k1[l], da_lambda_q2[l], da_lambda_k2[l],
                              da_subln_g[l], lambda_init)
        o_mla = mla_attention(c_q, c_kv, k_rope, mla_cos, mla_sin, mla_q_lora_g[l], mla_w_uq[l],
                              mla_kv_lora_g[l], mla_w_ukv[l], mla_q_norm_g[l],
                              mla_k_nope_norm_g[l], mla_k_rope_norm_g[l])
        gates = jax.nn.sigmoid(gate_logits.astype(jnp.float32).reshape(B, S, 2, D)
                               + b_gate[l].astype(jnp.float32))
        mixed = (gates[:, :, 0] * o_da.astype(jnp.float32)
                 + gates[:, :, 1] * o_mla.astype(jnp.float32)).astype(x.dtype)
        x = x + mixed @ w_o[l]
        h2 = rms_norm(x, ffn_norm_g[l])
        x = x + hier_moe(h2, w_group[l], b_group[l], w_router[l], b_router[l], w1[l], w3[l], w2[l])
    return x
```

```python
import functools
import math

import jax
import jax.numpy as jnp
from jax import lax
from jax.experimental import pallas as pl
from jax.experimental.pallas import tpu as pltpu

D_MODEL = 1024
DA_HEADS = 8
DA_HEAD_DIM = 64
DA_V_DIM = 128
DA_ROT_DIM = 16
MLA_HEADS = 8
MLA_Q_LORA = 384
MLA_KV_LORA = 256
MLA_NOPE = 128
MLA_ROPE = 64
MLA_V = 128
ROPE_THETA = 500000.0
N_GROUPS = 4
EXPERTS_PER_GROUP = 8
N_EXPERTS = N_GROUPS * EXPERTS_PER_GROUP
TOP_K = 2
D_EXPERT = 512
NORM_EPS = 1e-6

LANES = 128
SUBLANES = 8
ROW_CHUNKS = D_MODEL // LANES

PROJ_ROWS = 256
DA_TQ = 256
MLA_TQ = 512
ATT_TK = 256
MOE_CHUNK = 256
ROUTE_ROWS = 256
VMEM_LIMIT = 48 * 1024 * 1024

F32 = jnp.float32
BF16 = jnp.bfloat16


def _const_spec(shape):
    nd = len(shape)
    return pl.BlockSpec(shape, lambda *_: (0,) * nd, pipeline_mode=pl.Buffered(1))


def _rms(v, n):
    return lax.rsqrt(jnp.sum(v * v, axis=-1, keepdims=True) * (1.0 / n) + NORM_EPS)


def _dot(a, b):
    return jnp.dot(a, b, preferred_element_type=F32)


def _dot_nt(a, b):
    return lax.dot_general(a, b, (((1,), (1,)), ((), ())), preferred_element_type=F32)


def _in_proj_kernel(x_ref, g_ref, wq_ref, wk_ref, wv_ref, wcq_ref, wckv_ref, wkr_ref, wg_ref,
                    wuq_ref, wukv_ref, bg_ref, qg_ref, kg_ref, qlg_ref, kvlg_ref, mqn_ref, mqr_ref,
                    mkn_ref, mkr_ref, cda_ref, sda_ref, cm_ref, sm_ref,
                    q_ref, k_ref, v_ref, qn_ref, qr_ref, kn_ref, vm_ref, kr_ref, gate_ref):
    x = x_ref[...]
    h = (x * _rms(x, D_MODEL) * g_ref[...]).astype(BF16)
    rows = x.shape[0]
    lane = lax.broadcasted_iota(jnp.int32, (rows, LANES), 1)
    lo = lane < DA_HEAD_DIM
    da_first = (lane & (DA_HEAD_DIM - 1)) < (DA_ROT_DIM // 2)
    m_first = lane < (MLA_ROPE // 2)
    cda, sda = cda_ref[...], sda_ref[...]
    cm, sm = cm_ref[...], sm_ref[...]

    def da_rope(y):
        partner = jnp.where(da_first, pltpu.roll(y, LANES - DA_ROT_DIM // 2, 1),
                            pltpu.roll(y, DA_ROT_DIM // 2, 1))
        return y * cda + partner * sda

    def mla_rope(y):
        partner = jnp.where(m_first, pltpu.roll(y, LANES - MLA_ROPE // 2, 1),
                            pltpu.roll(y, MLA_ROPE // 2, 1))
        return y * cm + partner * sm

    for w_ref, gain_ref, o_ref in ((wq_ref, qg_ref, q_ref), (wk_ref, kg_ref, k_ref)):
        p_all = _dot(h, w_ref[...])
        gain = gain_ref[...]
        for hd in range(DA_HEADS):
            sl = slice(hd * LANES, (hd + 1) * LANES)
            p = p_all[:, sl]
            p2 = p * p
            s_all = jnp.sum(p2, axis=-1, keepdims=True)
            s_lo = jnp.sum(jnp.where(lo, p2, 0.0), axis=-1, keepdims=True)
            inv = jnp.where(lo, lax.rsqrt(s_lo * (1.0 / DA_HEAD_DIM) + NORM_EPS),
                            lax.rsqrt((s_all - s_lo) * (1.0 / DA_HEAD_DIM) + NORM_EPS))
            o_ref[:, sl] = da_rope(p * inv * gain).astype(BF16)
    v_ref[...] = _dot(h, wv_ref[...]).astype(BF16)

    cq = _dot(h, wcq_ref[...])
    cqn = (cq * _rms(cq, MLA_Q_LORA) * qlg_ref[...]).astype(BF16)
    qm = _dot(cqn, wuq_ref[...])
    gn, gr = mqn_ref[...], mqr_ref[...]
    for hd in range(MLA_HEADS):
        sl = slice(hd * LANES, (hd + 1) * LANES)
        qn = qm[:, sl]
        qr = qm[:, D_MODEL + hd * LANES:D_MODEL + (hd + 1) * LANES]
        ss = jnp.sum(qn * qn, axis=-1, keepdims=True) + jnp.sum(qr * qr, axis=-1, keepdims=True)
        inv = lax.rsqrt(ss * (1.0 / (MLA_NOPE + MLA_ROPE)) + NORM_EPS)
        qn_ref[:, sl] = (qn * inv * gn).astype(BF16)
        qr_ref[:, sl] = mla_rope(qr * inv * gr).astype(BF16)

    ckv = _dot(h, wckv_ref[...])
    ckvn = (ckv * _rms(ckv, MLA_KV_LORA) * kvlg_ref[...]).astype(BF16)
    kv = _dot(ckvn, wukv_ref[...])
    gkn = mkn_ref[...]
    for hd in range(MLA_HEADS):
        sl = slice(hd * LANES, (hd + 1) * LANES)
        kn = kv[:, sl]
        kn_ref[:, sl] = (kn * _rms(kn, MLA_NOPE) * gkn).astype(BF16)
    vm_ref[...] = kv[:, D_MODEL:].astype(BF16)
    kr = _dot(h, wkr_ref[...])
    kr_ref[...] = mla_rope(kr * _rms(kr, MLA_ROPE) * mkr_ref[...]).astype(BF16)

    gl = _dot(h, wg_ref[...]) + bg_ref[...]
    gate_ref[...] = (1.0 / (1.0 + jnp.exp(-gl))).astype(BF16)


def _in_proj(x2, g, wts, vecs, tables):
    T = x2.shape[0]
    tm = PROJ_ROWS
    row = lambda w: pl.BlockSpec((tm, w), lambda i: (i, 0))
    in_specs = ([row(D_MODEL), _const_spec(g.shape)] + [_const_spec(w.shape) for w in wts]
                + [_const_spec(v.shape) for v in vecs] + [row(LANES)] * 4)
    outs = [jax.ShapeDtypeStruct((T, D_MODEL), BF16)] * 7 + [
        jax.ShapeDtypeStruct((T, LANES), BF16), jax.ShapeDtypeStruct((T, 2 * D_MODEL), BF16)]
    out_specs = [row(D_MODEL)] * 7 + [row(LANES), row(2 * D_MODEL)]
    return pl.pallas_call(
        _in_proj_kernel, out_shape=outs, grid=(T // tm,), in_specs=in_specs, out_specs=out_specs,
        compiler_params=pltpu.CompilerParams(dimension_semantics=("parallel",),
                                             vmem_limit_bytes=VMEM_LIMIT),
        name="in_proj")(x2, g, *wts, *vecs, *tables)


def _flash(qs, load_kv, qi, tq, n_maps):
    rows = n_maps * tq
    tk = ATT_TK

    def step(j, carry, masked):
        m, l, acc = carry
        k, v = load_kv(j)
        s = _dot_nt(qs, k)
        if masked:
            qpos = qi * tq + (lax.broadcasted_iota(jnp.int32, (rows, tk), 0) & (tq - 1))
            kpos = j * tk + lax.broadcasted_iota(jnp.int32, (rows, tk), 1)
            s = jnp.where(kpos <= qpos, s, -jnp.inf)
        m_new = jnp.maximum(m, jnp.max(s, axis=-1, keepdims=True))
        alpha = jnp.exp(m - m_new)
        p = jnp.exp(s - m_new)
        l = alpha * l + jnp.sum(p, axis=-1, keepdims=True)
        acc = alpha * acc + _dot(p.astype(BF16), v)
        return m_new, l, acc

    init = (jnp.full((rows, 1), -jnp.inf, F32), jnp.zeros((rows, 1), F32),
            jnp.zeros((rows, LANES), F32))
    n_full = qi * (tq // tk)
    carry = lax.fori_loop(0, n_full, lambda j, c: step(j, c, False), init)
    for d in range(tq // tk):
        carry = step(n_full + d, carry, True)
    _, l, acc = carry
    return acc, l


def _da_attn_kernel(q_ref, k_ref, v_ref, lam_ref, sg_ref, o_ref, *, lambda_init):
    qi = pl.program_id(2)
    tq = DA_TQ
    q = q_ref[...]
    lane = lax.broadcasted_iota(jnp.int32, q.shape, 1)
    zero = jnp.zeros_like(q)
    qs = jnp.concatenate([jnp.where(lane < DA_HEAD_DIM, q, zero),
                          jnp.where(lane >= DA_HEAD_DIM, q, zero)], axis=0)

    def load_kv(j):
        sl = pl.ds(pl.multiple_of(j * ATT_TK, ATT_TK), ATT_TK)
        return k_ref[sl, :], v_ref[sl, :]

    acc, l = _flash(qs, load_kv, qi, tq, 2)
    o1 = acc[:tq] / l[:tq]
    o2 = acc[tq:] / l[tq:]
    lv = lam_ref[...]
    lam = (jnp.exp(jnp.sum(lv[0:1] * lv[1:2], axis=-1, keepdims=True))
           - jnp.exp(jnp.sum(lv[2:3] * lv[3:4], axis=-1, keepdims=True)) + lambda_init)
    o = o1 - lam * o2
    y = o * _rms(o, DA_V_DIM) * sg_ref[...] * (1.0 - lambda_init)
    o_ref[...] = y.astype(BF16)


def _da_attn(q, k, v, lam_vecs, subln_g, lambda_init):
    B, S, _ = q.shape
    tq = DA_TQ
    qspec = pl.BlockSpec((None, tq, LANES), lambda b, h, i: (b, i, h))
    kvspec = pl.BlockSpec((None, S, LANES), lambda b, h, i: (b, 0, h))
    return pl.pallas_call(
        functools.partial(_da_attn_kernel, lambda_init=lambda_init),
        out_shape=jax.ShapeDtypeStruct((B, S, D_MODEL), BF16),
        grid=(B, DA_HEADS, S // tq),
        in_specs=[qspec, kvspec, kvspec, _const_spec(lam_vecs.shape), _const_spec(subln_g.shape)],
        out_specs=qspec,
        compiler_params=pltpu.CompilerParams(
            dimension_semantics=("parallel", "parallel", "arbitrary"), vmem_limit_bytes=VMEM_LIMIT),
        name="da_attn")(q, k, v, lam_vecs, subln_g)


def _mla_attn_kernel(qn_ref, qr_ref, kn_ref, kr_ref, v_ref, o_ref):
    qi = pl.program_id(2)
    qs = jnp.concatenate([qn_ref[...], qr_ref[...]], axis=-1)

    def load_kv(j):
        sl = pl.ds(pl.multiple_of(j * ATT_TK, ATT_TK), ATT_TK)
        return jnp.concatenate([kn_ref[sl, :], kr_ref[sl, :]], axis=-1), v_ref[sl, :]

    acc, l = _flash(qs, load_kv, qi, MLA_TQ, 1)
    o_ref[...] = (acc / l).astype(BF16)


def _mla_attn(qn, qr, kn, kr, vm):
    B, S, _ = qn.shape
    tq = MLA_TQ
    qspec = pl.BlockSpec((None, tq, LANES), lambda b, h, i: (b, i, h))
    kvspec = pl.BlockSpec((None, S, LANES), lambda b, h, i: (b, 0, h))
    krspec = pl.BlockSpec((None, S, LANES), lambda b, h, i: (b, 0, 0))
    return pl.pallas_call(
        _mla_attn_kernel,
        out_shape=jax.ShapeDtypeStruct((B, S, D_MODEL), BF16),
        grid=(B, MLA_HEADS, S // tq),
        in_specs=[qspec, qspec, kvspec, krspec, kvspec],
        out_specs=qspec,
        compiler_params=pltpu.CompilerParams(
            dimension_semantics=("parallel", "parallel", "arbitrary"), vmem_limit_bytes=VMEM_LIMIT),
        name="mla_attn")(qn, qr, kn, kr, vm)


def _merge_kernel(oda_ref, omla_ref, gate_ref, x_ref, wo_ref, fg_ref, whi_ref, wlo_ref, br_ref,
                  x1_ref, h2_ref, ri_ref, rw_ref, cnt_ref, carry_ref):
    i = pl.program_id(0)
    rows = x_ref.shape[0]

    @pl.when(i == 0)
    def _():
        carry_ref[...] = jnp.zeros_like(carry_ref)

    g = gate_ref[...].astype(F32)
    mixed = g[:, :D_MODEL] * oda_ref[...].astype(F32) + g[:, D_MODEL:] * omla_ref[...].astype(F32)
    x1 = x_ref[...] + _dot(mixed.astype(BF16), wo_ref[...])
    x1_ref[...] = x1
    h2 = x1 * _rms(x1, D_MODEL) * fg_ref[...]
    for s in range(ROW_CHUNKS):
        h2_ref[pl.ds(s, rows, stride=ROW_CHUNKS), :] = h2[:, s * LANES:(s + 1) * LANES]

    hi = h2.astype(BF16)
    lo = (h2 - hi.astype(F32)).astype(BF16)
    whi = whi_ref[...]
    logits = _dot(hi, whi) + _dot(hi, wlo_ref[...]) + _dot(lo, whi) + br_ref[...]

    lane = lax.broadcasted_iota(jnp.int32, (rows, LANES), 1)
    neg = -jnp.inf
    gl = jnp.where(lane < N_GROUPS, logits, neg)
    gmax = jnp.max(gl, axis=-1, keepdims=True)
    g_sel = jnp.min(jnp.where(gl == gmax, lane, LANES), axis=-1, keepdims=True)
    p_g = 1.0 / jnp.sum(jnp.exp(gl - gmax), axis=-1, keepdims=True)
    e_lo = N_GROUPS + g_sel * EXPERTS_PER_GROUP
    el = jnp.where((lane >= e_lo) & (lane < e_lo + EXPERTS_PER_GROUP), logits, neg)
    v0 = jnp.max(el, axis=-1, keepdims=True)
    i0 = jnp.min(jnp.where(el == v0, lane, LANES), axis=-1, keepdims=True)
    el2 = jnp.where(lane == i0, neg, el)
    v1 = jnp.max(el2, axis=-1, keepdims=True)
    i1 = jnp.min(jnp.where(el2 == v1, lane, LANES), axis=-1, keepdims=True)
    t = jnp.exp(v1 - v0)
    w0 = p_g / (1.0 + t)
    w1 = p_g * t / (1.0 + t)
    e0 = i0 - N_GROUPS
    e1 = i1 - N_GROUPS

    oh = jnp.where((lane == e0) | (lane == e1), 1.0, 0.0)
    r_i = lax.broadcasted_iota(jnp.int32, (rows, rows), 0)
    c_i = lax.broadcasted_iota(jnp.int32, (rows, rows), 1)
    tri = jnp.where(c_i < r_i, 1.0, 0.0).astype(BF16)
    before = _dot(tri, oh.astype(BF16)) + carry_ref[...]
    rank0 = jnp.sum(jnp.where(lane == e0, before, 0.0), axis=-1, keepdims=True).astype(jnp.int32)
    rank1 = jnp.sum(jnp.where(lane == e1, before, 0.0), axis=-1, keepdims=True).astype(jnp.int32)
    carry = carry_ref[...] + jnp.sum(oh, axis=0, keepdims=True)
    carry_ref[...] = carry
    cnt_ref[...] = carry

    ri = jnp.where(lane == 0, e0, jnp.where(lane == 1, e1, jnp.where(lane == 2, rank0, rank1)))
    ri_ref[...] = ri[:, :SUBLANES]
    rw_ref[...] = jnp.where(lane == 0, w0, w1)[:, :SUBLANES]


def _merge(oda, omla, gates, x2, wo, fg, whi, wlo, br):
    T = x2.shape[0]
    tm = PROJ_ROWS
    row = lambda w: pl.BlockSpec((tm, w), lambda i: (i, 0))
    outs = [jax.ShapeDtypeStruct((T, D_MODEL), F32),
            jax.ShapeDtypeStruct((T * ROW_CHUNKS, LANES), F32),
            jax.ShapeDtypeStruct((T, SUBLANES), jnp.int32),
            jax.ShapeDtypeStruct((T, SUBLANES), F32),
            jax.ShapeDtypeStruct((1, LANES), F32)]
    out_specs = [row(D_MODEL), pl.BlockSpec((tm * ROW_CHUNKS, LANES), lambda i: (i, 0)),
                 row(SUBLANES), row(SUBLANES), pl.BlockSpec((1, LANES), lambda i: (0, 0))]
    return pl.pallas_call(
        _merge_kernel, out_shape=outs, grid=(T // tm,),
        in_specs=[row(D_MODEL), row(D_MODEL), row(2 * D_MODEL), row(D_MODEL), _const_spec(wo.shape),
                  _const_spec(fg.shape), _const_spec(whi.shape), _const_spec(wlo.shape),
                  _const_spec(br.shape)],
        out_specs=out_specs,
        scratch_shapes=[pltpu.VMEM((1, LANES), F32)],
        compiler_params=pltpu.CompilerParams(dimension_semantics=("arbitrary",),
                                             vmem_limit_bytes=VMEM_LIMIT),
        name="merge_router")(oda, omla, gates, x2, wo, fg, whi, wlo, br)


def _row(ref, r):
    return ref.at[pl.ds(pl.multiple_of(r * ROW_CHUNKS, ROW_CHUNKS), ROW_CHUNKS), :]


def _dispatch_kernel(pad_ref, dest_ref, h2_ref, xb_ref, zero_ref, sem, zsem):
    i = pl.program_id(0)
    tb = ROUTE_ROWS
    n_pad = pad_ref.shape[0]

    @pl.when(i == 0)
    def _():
        zero_ref[...] = jnp.zeros_like(zero_ref)

        def issue(r, c):
            pltpu.make_async_copy(zero_ref, _row(xb_ref, pad_ref[r]), zsem).start()
            return c

        lax.fori_loop(0, n_pad, issue, 0)

        def drain(r, c):
            pltpu.make_async_copy(zero_ref, _row(xb_ref, 0), zsem).wait()
            return c

        lax.fori_loop(0, n_pad, drain, 0)

    def issue_tok(t, c):
        src = _row(h2_ref, i * tb + t)
        pltpu.make_async_copy(src, _row(xb_ref, dest_ref[0, 0, 2 * t]), sem).start()
        pltpu.make_async_copy(src, _row(xb_ref, dest_ref[0, 0, 2 * t + 1]), sem).start()
        return c

    lax.fori_loop(0, tb, issue_tok, 0)

    def drain_tok(t, c):
        pltpu.make_async_copy(_row(h2_ref, 0), _row(xb_ref, 0), sem).wait()
        return c

    lax.fori_loop(0, 2 * tb, drain_tok, 0)


def _dispatch(pad_rows, dest3, h2r, n_rows):
    nblk = dest3.shape[0]
    return pl.pallas_call(
        _dispatch_kernel,
        out_shape=jax.ShapeDtypeStruct((n_rows * ROW_CHUNKS, LANES), F32),
        grid_spec=pltpu.PrefetchScalarGridSpec(
            num_scalar_prefetch=1, grid=(nblk,),
            in_specs=[pl.BlockSpec((1, 1, 2 * ROUTE_ROWS), lambda i, p: (i, 0, 0),
                                   memory_space=pltpu.SMEM),
                      pl.BlockSpec(memory_space=pl.ANY)],
            out_specs=pl.BlockSpec(memory_space=pl.ANY),
            scratch_shapes=[pltpu.VMEM((ROW_CHUNKS, LANES), F32), pltpu.SemaphoreType.DMA,
                            pltpu.SemaphoreType.DMA]),
        compiler_params=pltpu.CompilerParams(dimension_semantics=("arbitrary",),
                                             has_side_effects=True),
        name="dispatch")(pad_rows, dest3, h2r)


def _expert_kernel(ce_ref, nu_ref, xb_ref, w1_ref, w3_ref, w2_ref, yb_ref):
    c = pl.program_id(0)
    ch = MOE_CHUNK

    @pl.when(c < nu_ref[0])
    def _():
        x = jnp.concatenate([xb_ref[pl.ds(s, ch, stride=ROW_CHUNKS), :] for s in range(ROW_CHUNKS)],
                            axis=-1).astype(BF16)
        a = _dot(x, w1_ref[...])
        b = _dot(x, w3_ref[...])
        hmid = (a / (1.0 + jnp.exp(-a))) * b
        y = _dot(hmid.astype(BF16), w2_ref[...])
        for s in range(ROW_CHUNKS):
            yb_ref[pl.ds(s, ch, stride=ROW_CHUNKS), :] = y[:, s * LANES:(s + 1) * LANES]

    @pl.when(c >= nu_ref[0])
    def _():
        yb_ref[...] = jnp.zeros_like(yb_ref)


def _experts(chunk_e, n_used, xb, w1, w3, w2, n_chunks):
    ch = MOE_CHUNK
    xspec = pl.BlockSpec((ch * ROW_CHUNKS, LANES), lambda c, ce, nu: (c, 0))
    wspec = lambda s: pl.BlockSpec((None,) + s, lambda c, ce, nu: (ce[c], 0, 0))
    return pl.pallas_call(
        _expert_kernel,
        out_shape=jax.ShapeDtypeStruct((n_chunks * ch * ROW_CHUNKS, LANES), F32),
        grid_spec=pltpu.PrefetchScalarGridSpec(
            num_scalar_prefetch=2, grid=(n_chunks,),
            in_specs=[xspec, wspec((D_MODEL, D_EXPERT)), wspec((D_MODEL, D_EXPERT)),
                      wspec((D_EXPERT, D_MODEL))],
            out_specs=xspec),
        compiler_params=pltpu.CompilerParams(dimension_semantics=("arbitrary",),
                                             vmem_limit_bytes=VMEM_LIMIT),
        name="experts")(chunk_e, n_used, xb, w1, w3, w2)


def _combine_kernel(dest_ref, yb_ref, x1_ref, rw_ref, o_ref, buf_ref, sem):
    tb = ROUTE_ROWS

    def issue(t, c):
        for k in range(TOP_K):
            pltpu.make_async_copy(_row(yb_ref, dest_ref[0, 0, 2 * t + k]),
                                  _row(buf_ref.at[k], t), sem).start()
        return c

    lax.fori_loop(0, tb, issue, 0)

    def drain(t, c):
        pltpu.make_async_copy(_row(yb_ref, 0), _row(buf_ref.at[0], 0), sem).wait()
        return c

    lax.fori_loop(0, TOP_K * tb, drain, 0)
    rw = rw_ref[...]
    acc = x1_ref[...]
    for k in range(TOP_K):
        y = jnp.concatenate([buf_ref[k, pl.ds(s, tb, stride=ROW_CHUNKS), :] for s in range(ROW_CHUNKS)],
                            axis=-1)
        acc = acc + y * rw[:, k:k + 1]
    o_ref[...] = acc


def _combine(dest3, yb, x1, rw):
    T = x1.shape[0]
    tb = ROUTE_ROWS
    return pl.pallas_call(
        _combine_kernel,
        out_shape=jax.ShapeDtypeStruct((T, D_MODEL), F32),
        grid=(T // tb,),
        in_specs=[pl.BlockSpec((1, 1, 2 * tb), lambda i: (i, 0, 0), memory_space=pltpu.SMEM),
                  pl.BlockSpec(memory_space=pl.ANY),
                  pl.BlockSpec((tb, D_MODEL), lambda i: (i, 0)),
                  pl.BlockSpec((tb, SUBLANES), lambda i: (i, 0))],
        out_specs=pl.BlockSpec((tb, D_MODEL), lambda i: (i, 0)),
        scratch_shapes=[pltpu.VMEM((TOP_K, tb * ROW_CHUNKS, LANES), F32), pltpu.SemaphoreType.DMA],
        compiler_params=pltpu.CompilerParams(dimension_semantics=("arbitrary",),
                                             vmem_limit_bytes=VMEM_LIMIT),
        name="combine")(dest3, yb, x1, rw)


def _rope_tables(positions, dim, half_lanes, one_fill):
    inv = ROPE_THETA ** (-jnp.arange(0, dim, 2, dtype=F32) / dim)
    ang = positions.astype(F32).reshape(-1, 1) * inv
    c, s = jnp.cos(ang), jnp.sin(ang)
    T = ang.shape[0]
    fill = half_lanes - dim
    cseg = jnp.concatenate([c, c, jnp.full((T, fill), one_fill, F32)], axis=-1)
    sseg = jnp.concatenate([-s, s, jnp.zeros((T, fill), F32)], axis=-1)
    reps = LANES // half_lanes
    return jnp.tile(cseg, (1, reps)), jnp.tile(sseg, (1, reps))


def _layer(x2, B, S, tables, lambda_init, attn_norm_g, w_in, b_gate, da_q_norm_g, da_k_norm_g,
           da_lambda_q1, da_lambda_k1, da_lambda_q2, da_lambda_k2, da_subln_g, mla_q_lora_g,
           mla_w_uq, mla_kv_lora_g, mla_w_ukv, mla_q_norm_g, mla_k_nope_norm_g, mla_k_rope_norm_g,
           w_o, ffn_norm_g, w_group, b_group, w_router, b_router, w1, w3, w2):
    T = B * S
    D = D_MODEL
    c0, c1, c2 = D, 2 * D, 3 * D
    c3 = c2 + MLA_Q_LORA
    c4 = c3 + MLA_KV_LORA
    c5 = c4 + MLA_ROPE
    wb = w_in.astype(BF16)
    wkr = jnp.pad(wb[:, c4:c5], ((0, 0), (0, LANES - MLA_ROPE)))
    qk = MLA_NOPE + MLA_ROPE
    uq = mla_w_uq.astype(BF16).reshape(MLA_Q_LORA, MLA_HEADS, qk)
    wuq = jnp.concatenate([
        uq[:, :, :MLA_NOPE].reshape(MLA_Q_LORA, D),
        jnp.pad(uq[:, :, MLA_NOPE:], ((0, 0), (0, 0), (0, LANES - MLA_ROPE))).reshape(MLA_Q_LORA, D)],
        axis=1)
    ukv = mla_w_ukv.astype(BF16).reshape(MLA_KV_LORA, MLA_HEADS, MLA_NOPE + MLA_V)
    wukv = jnp.concatenate([ukv[:, :, :MLA_NOPE].reshape(MLA_KV_LORA, D),
                            ukv[:, :, MLA_NOPE:].reshape(MLA_KV_LORA, D)], axis=1)
    wts = (wb[:, :c0], wb[:, c0:c1], wb[:, c1:c2], wb[:, c2:c3], wb[:, c3:c4], wkr, wb[:, c5:],
           wuq, wukv)
    r1 = lambda v: v.astype(F32).reshape(1, -1)
    pad_rope = lambda v: jnp.pad(v.astype(F32), (0, LANES - MLA_ROPE)).reshape(1, LANES)
    vecs = (r1(b_gate),
            jnp.tile(r1(da_q_norm_g), (1, 2)) * (DA_HEAD_DIM ** -0.5),
            jnp.tile(r1(da_k_norm_g), (1, 2)),
            r1(mla_q_lora_g), r1(mla_kv_lora_g),
            r1(mla_q_norm_g[:MLA_NOPE]) * (qk ** -0.5),
            pad_rope(mla_q_norm_g[MLA_NOPE:]) * (qk ** -0.5),
            r1(mla_k_nope_norm_g), pad_rope(mla_k_rope_norm_g))
    q, k, v, qn, qr, kn, vm, kr, gates = _in_proj(x2, r1(attn_norm_g), wts, vecs, tables)

    b3 = lambda a: a.reshape(B, S, a.shape[-1])
    lam_vecs = jnp.stack([da_lambda_q1, da_lambda_k1, da_lambda_q2, da_lambda_k2]).astype(F32)
    o_da = _da_attn(b3(q), b3(k), b3(v), lam_vecs, r1(da_subln_g), lambda_init)
    o_mla = _mla_attn(b3(qn), b3(qr), b3(kn), b3(kr), b3(vm))

    wgr = jnp.concatenate([w_group.astype(F32), w_router.astype(F32),
                           jnp.zeros((D, LANES - N_GROUPS - N_EXPERTS), F32)], axis=1)
    whi = wgr.astype(BF16)
    wlo = (wgr - whi.astype(F32)).astype(BF16)
    br = jnp.concatenate([b_group.astype(F32), b_router.astype(F32),
                          jnp.zeros((LANES - N_GROUPS - N_EXPERTS,), F32)]).reshape(1, LANES)
    x1, h2r, ri, rw, cnt = _merge(o_da.reshape(T, D), o_mla.reshape(T, D), gates, x2,
                                  w_o.astype(BF16), r1(ffn_norm_g), whi, wlo, br)

    ch = MOE_CHUNK
    counts = cnt[0, :N_EXPERTS].astype(jnp.int32)
    padded = (counts + ch - 1) // ch * ch
    pends = jnp.cumsum(padded)
    pstarts = pends - padded
    n_chunks = (T * TOP_K) // ch + N_EXPERTS
    P = n_chunks * ch
    dest = pstarts[ri[:, :TOP_K]] + ri[:, TOP_K:2 * TOP_K]
    dest3 = dest.reshape(T // ROUTE_ROWS, 1, TOP_K * ROUTE_ROWS)
    chunk_e = jnp.minimum(jnp.searchsorted(pends, jnp.arange(n_chunks, dtype=jnp.int32) * ch,
                                           side='right'), N_EXPERTS - 1).astype(jnp.int32)
    n_used = (pends[-1:] // ch).astype(jnp.int32)
    r = jnp.arange(ch, dtype=jnp.int32)[None, :]
    e_i = jnp.arange(N_EXPERTS, dtype=jnp.int32)[:, None]
    pad_rows = jnp.where(r < (padded - counts)[:, None], (pstarts + counts)[:, None] + r,
                         P + e_i * ch + r).reshape(-1).astype(jnp.int32)

    xb = _dispatch(pad_rows, dest3, h2r, P + N_EXPERTS * ch)
    yb = _experts(chunk_e, n_used, xb, w1.astype(BF16), w3.astype(BF16), w2.astype(BF16), n_chunks)
    return _combine(dest3, yb, x1, rw)


def kernel(x, positions, attn_norm_g, w_in, b_gate, da_q_norm_g, da_k_norm_g, da_lambda_q1, da_lambda_k1, da_lambda_q2, da_lambda_k2, da_subln_g, mla_q_lora_g, mla_w_uq, mla_kv_lora_g, mla_w_ukv, mla_q_norm_g, mla_k_nope_norm_g, mla_k_rope_norm_g, w_o, ffn_norm_g, w_group, b_group, w_router, b_router, w1, w3, w2):
    B, S, D = x.shape
    assert D == D_MODEL and S % MLA_TQ == 0 and (B * S) % PROJ_ROWS == 0
    tables = (_rope_tables(positions, DA_ROT_DIM, DA_HEAD_DIM, 1.0)
              + _rope_tables(positions, MLA_ROPE, LANES, 1.0))
    x2 = x.reshape(B * S, D)
    per_layer = (attn_norm_g, w_in, b_gate, da_q_norm_g, da_k_norm_g, da_lambda_q1, da_lambda_k1,
                 da_lambda_q2, da_lambda_k2, da_subln_g, mla_q_lora_g, mla_w_uq, mla_kv_lora_g,
                 mla_w_ukv, mla_q_norm_g, mla_k_nope_norm_g, mla_k_rope_norm_g, w_o, ffn_norm_g,
                 w_group, b_group, w_router, b_router, w1, w3, w2)
    for l in range(w_in.shape[0]):
        lambda_init = 0.8 - 0.6 * math.exp(-0.3 * l)
        x2 = _layer(x2, B, S, tables, lambda_init, *[p[l] for p in per_layer])
    return x2.reshape(B, S, D)
```

```python
import functools
import math

import jax
import jax.numpy as jnp
from jax import lax
from jax.experimental import pallas as pl
from jax.experimental.pallas import tpu as pltpu

D_MODEL = 1024
DA_HEADS = 8
DA_HEAD_DIM = 64
DA_V_DIM = 128
DA_ROT_DIM = 16
MLA_HEADS = 8
MLA_Q_LORA = 384
MLA_KV_LORA = 256
MLA_NOPE = 128
MLA_ROPE = 64
MLA_V = 128
ROPE_THETA = 500000.0
N_GROUPS = 4
EXPERTS_PER_GROUP = 8
N_EXPERTS = N_GROUPS * EXPERTS_PER_GROUP
TOP_K = 2
D_EXPERT = 512
NORM_EPS = 1e-6

LANES = 128
SUBLANES = 8
ROW_CHUNKS = D_MODEL // LANES

PROJ_ROWS = 256
DA_TQ = 256
MLA_TQ = 512
ATT_TK = 256
MOE_CHUNK = 256
ROUTE_ROWS = 256
VMEM_LIMIT = 48 * 1024 * 1024

F32 = jnp.float32
BF16 = jnp.bfloat16


def _const_spec(shape):
    nd = len(shape)
    return pl.BlockSpec(shape, lambda *_: (0,) * nd, pipeline_mode=pl.Buffered(1))


def _rms(v, n):
    return lax.rsqrt(jnp.sum(v * v, axis=-1, keepdims=True) * (1.0 / n) + NORM_EPS)


def _dot(a, b):
    return jnp.dot(a, b, preferred_element_type=F32)


def _dot_nt(a, b):
    return lax.dot_general(a, b, (((1,), (1,)), ((), ())), preferred_element_type=F32)


def _in_proj_kernel(x_ref, g_ref, wq_ref, wk_ref, wv_ref, wcq_ref, wckv_ref, wkr_ref, wg_ref,
                    wuq_ref, wukv_ref, bg_ref, qg_ref, kg_ref, qlg_ref, kvlg_ref, mqn_ref, mqr_ref,
                    mkn_ref, mkr_ref, cda_ref, sda_ref, cm_ref, sm_ref,
                    q_ref, k_ref, v_ref, qn_ref, qr_ref, kn_ref, vm_ref, kr_ref, gate_ref):
    x = x_ref[...]
    h = (x * _rms(x, D_MODEL) * g_ref[...]).astype(BF16)
    rows = x.shape[0]
    lane = lax.broadcasted_iota(jnp.int32, (rows, LANES), 1)
    lo = lane < DA_HEAD_DIM
    da_first = (lane & (DA_HEAD_DIM - 1)) < (DA_ROT_DIM // 2)
    m_first = lane < (MLA_ROPE // 2)
    cda, sda = cda_ref[...], sda_ref[...]
    cm, sm = cm_ref[...], sm_ref[...]

    def da_rope(y):
        partner = jnp.where(da_first, pltpu.roll(y, LANES - DA_ROT_DIM // 2, 1),
                            pltpu.roll(y, DA_ROT_DIM // 2, 1))
        return y * cda + partner * sda

    def mla_rope(y):
        partner = jnp.where(m_first, pltpu.roll(y, LANES - MLA_ROPE // 2, 1),
                            pltpu.roll(y, MLA_ROPE // 2, 1))
        return y * cm + partner * sm

    for w_ref, gain_ref, o_ref in ((wq_ref, qg_ref, q_ref), (wk_ref, kg_ref, k_ref)):
        p_all = _dot(h, w_ref[...])
        gain = gain_ref[...]
        for hd in range(DA_HEADS):
            sl = slice(hd * LANES, (hd + 1) * LANES)
            p = p_all[:, sl]
            p2 = p * p
            s_all = jnp.sum(p2, axis=-1, keepdims=True)
            s_lo = jnp.sum(jnp.where(lo, p2, 0.0), axis=-1, keepdims=True)
            inv = jnp.where(lo, lax.rsqrt(s_lo * (1.0 / DA_HEAD_DIM) + NORM_EPS),
                            lax.rsqrt((s_all - s_lo) * (1.0 / DA_HEAD_DIM) + NORM_EPS))
            o_ref[:, sl] = da_rope(p * inv * gain).astype(BF16)
    v_ref[...] = _dot(h, wv_ref[...]).astype(BF16)

    cq = _dot(h, wcq_ref[...])
    cqn = (cq * _rms(cq, MLA_Q_LORA) * qlg_ref[...]).astype(BF16)
    qm = _dot(cqn, wuq_ref[...])
    gn, gr = mqn_ref[...], mqr_ref[...]
    for hd in range(MLA_HEADS):
        sl = slice(hd * LANES, (hd + 1) * LANES)
        qn = qm[:, sl]
        qr = qm[:, D_MODEL + hd * LANES:D_MODEL + (hd + 1) * LANES]
        ss = jnp.sum(qn * qn, axis=-1, keepdims=True) + jnp.sum(qr * qr, axis=-1, keepdims=True)
        inv = lax.rsqrt(ss * (1.0 / (MLA_NOPE + MLA_ROPE)) + NORM_EPS)
        qn_ref[:, sl] = (qn * inv * gn).astype(BF16)
        qr_ref[:, sl] = mla_rope(qr * inv * gr).astype(BF16)

    ckv = _dot(h, wckv_ref[...])
    ckvn = (ckv * _rms(ckv, MLA_KV_LORA) * kvlg_ref[...]).astype(BF16)
    kv = _dot(ckvn, wukv_ref[...])
    gkn = mkn_ref[...]
    for hd in range(MLA_HEADS):
        sl = slice(hd * LANES, (hd + 1) * LANES)
        kn = kv[:, sl]
        kn_ref[:, sl] = (kn * _rms(kn, MLA_NOPE) * gkn).astype(BF16)
    vm_ref[...] = kv[:, D_MODEL:].astype(BF16)
    kr = _dot(h, wkr_ref[...])
    kr_ref[...] = mla_rope(kr * _rms(kr, MLA_ROPE) * mkr_ref[...]).astype(BF16)

    gl = _dot(h, wg_ref[...]) + bg_ref[...]
    gate_ref[...] = (1.0 / (1.0 + jnp.exp(-gl))).astype(BF16)


def _in_proj(x2, g, wts, vecs, tables):
    T = x2.shape[0]
    tm = PROJ_ROWS
    row = lambda w: pl.BlockSpec((tm, w), lambda i: (i, 0))
    in_specs = ([row(D_MODEL), _const_spec(g.shape)] + [_const_spec(w.shape) for w in wts]
                + [_const_spec(v.shape) for v in vecs] + [row(LANES)] * 4)
    outs = [jax.ShapeDtypeStruct((T, D_MODEL), BF16)] * 7 + [
        jax.ShapeDtypeStruct((T, LANES), BF16), jax.ShapeDtypeStruct((T, 2 * D_MODEL), BF16)]
    out_specs = [row(D_MODEL)] * 7 + [row(LANES), row(2 * D_MODEL)]
    return pl.pallas_call(
        _in_proj_kernel, out_shape=outs, grid=(T // tm,), in_specs=in_specs, out_specs=out_specs,
        compiler_params=pltpu.CompilerParams(dimension_semantics=("parallel",),
                                             vmem_limit_bytes=VMEM_LIMIT),
        name="in_proj")(x2, g, *wts, *vecs, *tables)


def _flash(qs, load_kv, qi, tq, n_maps):
    rows = n_maps * tq
    tk = ATT_TK

    def step(j, carry, masked):
        m, l, acc = carry
        k, v = load_kv(j)
        s = _dot_nt(qs, k)
        if masked:
            qpos = qi * tq + (lax.broadcasted_iota(jnp.int32, (rows, tk), 0) & (tq - 1))
            kpos = j * tk + lax.broadcasted_iota(jnp.int32, (rows, tk), 1)
            s = jnp.where(kpos <= qpos, s, -jnp.inf)
        m_new = jnp.maximum(m, jnp.max(s, axis=-1, keepdims=True))
        alpha = jnp.exp(m - m_new)
        p = jnp.exp(s - m_new)
        l = alpha * l + jnp.sum(p, axis=-1, keepdims=True)
        acc = alpha * acc + _dot(p.astype(BF16), v)
        return m_new, l, acc

    init = (jnp.full((rows, 1), -jnp.inf, F32), jnp.zeros((rows, 1), F32),
            jnp.zeros((rows, LANES), F32))
    n_full = qi * (tq // tk)
    carry = lax.fori_loop(0, n_full, lambda j, c: step(j, c, False), init)
    for d in range(tq // tk):
        carry = step(n_full + d, carry, True)
    _, l, acc = carry
    return acc, l


def _da_attn_kernel(q_ref, k_ref, v_ref, lam_ref, sg_ref, o_ref, *, lambda_init):
    qi = pl.program_id(2)
    tq = DA_TQ
    q = q_ref[...]
    lane = lax.broadcasted_iota(jnp.int32, q.shape, 1)
    zero = jnp.zeros_like(q)
    qs = jnp.concatenate([jnp.where(lane < DA_HEAD_DIM, q, zero),
                          jnp.where(lane >= DA_HEAD_DIM, q, zero)], axis=0)

    def load_kv(j):
        sl = pl.ds(pl.multiple_of(j * ATT_TK, ATT_TK), ATT_TK)
        return k_ref[sl, :], v_ref[sl, :]

    acc, l = _flash(qs, load_kv, qi, tq, 2)
    o1 = acc[:tq] / l[:tq]
    o2 = acc[tq:] / l[tq:]
    lv = lam_ref[...]
    lam = (jnp.exp(jnp.sum(lv[0:1] * lv[1:2], axis=-1, keepdims=True))
           - jnp.exp(jnp.sum(lv[2:3] * lv[3:4], axis=-1, keepdims=True)) + lambda_init)
    o = o1 - lam * o2
    y = o * _rms(o, DA_V_DIM) * sg_ref[...] * (1.0 - lambda_init)
    o_ref[...] = y.astype(BF16)


def _da_attn(q, k, v, lam_vecs, subln_g, lambda_init):
    B, S, _ = q.shape
    tq = DA_TQ
    qspec = pl.BlockSpec((None, tq, LANES), lambda b, h, i: (b, i, h))
    kvspec = pl.BlockSpec((None, S, LANES), lambda b, h, i: (b, 0, h))
    return pl.pallas_call(
        functools.partial(_da_attn_kernel, lambda_init=lambda_init),
        out_shape=jax.ShapeDtypeStruct((B, S, D_MODEL), BF16),
        grid=(B, DA_HEADS, S // tq),
        in_specs=[qspec, kvspec, kvspec, _const_spec(lam_vecs.shape), _const_spec(subln_g.shape)],
        out_specs=qspec,
        compiler_params=pltpu.CompilerParams(
            dimension_semantics=("parallel", "parallel", "arbitrary"), vmem_limit_bytes=VMEM_LIMIT),
        name="da_attn")(q, k, v, lam_vecs, subln_g)


def _mla_attn_kernel(qn_ref, qr_ref, kn_ref, kr_ref, v_ref, o_ref):
    qi = pl.program_id(2)
    qs = jnp.concatenate([qn_ref[...], qr_ref[...]], axis=-1)

    def load_kv(j):
        sl = pl.ds(pl.multiple_of(j * ATT_TK, ATT_TK), ATT_TK)
        return jnp.concatenate([kn_ref[sl, :], kr_ref[sl, :]], axis=-1), v_ref[sl, :]

    acc, l = _flash(qs, load_kv, qi, MLA_TQ, 1)
    o_ref[...] = (acc / l).astype(BF16)


def _mla_attn(qn, qr, kn, kr, vm):
    B, S, _ = qn.shape
    tq = MLA_TQ
    qspec = pl.BlockSpec((None, tq, LANES), lambda b, h, i: (b, i, h))
    kvspec = pl.BlockSpec((None, S, LANES), lambda b, h, i: (b, 0, h))
    krspec = pl.BlockSpec((None, S, LANES), lambda b, h, i: (b, 0, 0))
    return pl.pallas_call(
        _mla_attn_kernel,
        out_shape=jax.ShapeDtypeStruct((B, S, D_MODEL), BF16),
        grid=(B, MLA_HEADS, S // tq),
        in_specs=[qspec, qspec, kvspec, krspec, kvspec],
        out_specs=qspec,
        compiler_params=pltpu.CompilerParams(
            dimension_semantics=("parallel", "parallel", "arbitrary"), vmem_limit_bytes=VMEM_LIMIT),
        name="mla_attn")(qn, qr, kn, kr, vm)


def _merge_kernel(oda_ref, omla_ref, gate_ref, x_ref, wo_ref, fg_ref, whi_ref, wlo_ref, br_ref,
                  x1_ref, h2_ref, ri_ref, rw_ref, cnt_ref, carry_ref):
    i = pl.program_id(0)
    rows = x_ref.shape[0]

    @pl.when(i == 0)
    def _():
        carry_ref[...] = jnp.zeros_like(carry_ref)

    g = gate_ref[...].astype(F32)
    mixed = g[:, :D_MODEL] * oda_ref[...].astype(F32) + g[:, D_MODEL:] * omla_ref[...].astype(F32)
    x1 = x_ref[...] + _dot(mixed.astype(BF16), wo_ref[...])
    x1_ref[...] = x1
    h2 = x1 * _rms(x1, D_MODEL) * fg_ref[...]
    for s in range(ROW_CHUNKS):
        h2_ref[pl.ds(s, rows, stride=ROW_CHUNKS), :] = h2[:, s * LANES:(s + 1) * LANES]

    hi = h2.astype(BF16)
    lo = (h2 - hi.astype(F32)).astype(BF16)
    whi = whi_ref[...]
    logits = _dot(hi, whi) + _dot(hi, wlo_ref[...]) + _dot(lo, whi) + br_ref[...]

    lane = lax.broadcasted_iota(jnp.int32, (rows, LANES), 1)
    neg = -jnp.inf
    gl = jnp.where(lane < N_GROUPS, logits, neg)
    gmax = jnp.max(gl, axis=-1, keepdims=True)
    g_sel = jnp.min(jnp.where(gl == gmax, lane, LANES), axis=-1, keepdims=True)
    p_g = 1.0 / jnp.sum(jnp.exp(gl - gmax), axis=-1, keepdims=True)
    e_lo = N_GROUPS + g_sel * EXPERTS_PER_GROUP
    el = jnp.where((lane >= e_lo) & (lane < e_lo + EXPERTS_PER_GROUP), logits, neg)
    v0 = jnp.max(el, axis=-1, keepdims=True)
    i0 = jnp.min(jnp.where(el == v0, lane, LANES), axis=-1, keepdims=True)
    el2 = jnp.where(lane == i0, neg, el)
    v1 = jnp.max(el2, axis=-1, keepdims=True)
    i1 = jnp.min(jnp.where(el2 == v1, lane, LANES), axis=-1, keepdims=True)
    t = jnp.exp(v1 - v0)
    w0 = p_g / (1.0 + t)
    w1 = p_g * t / (1.0 + t)
    e0 = i0 - N_GROUPS
    e1 = i1 - N_GROUPS

    oh = jnp.where((lane == e0) | (lane == e1), 1.0, 0.0)
    r_i = lax.broadcasted_iota(jnp.int32, (rows, rows), 0)
    c_i = lax.broadcasted_iota(jnp.int32, (rows, rows), 1)
    tri = jnp.where(c_i < r_i, 1.0, 0.0).astype(BF16)
    before = _dot(tri, oh.astype(BF16)) + carry_ref[...]
    rank0 = jnp.sum(jnp.where(lane == e0, before, 0.0), axis=-1, keepdims=True).astype(jnp.int32)
    rank1 = jnp.sum(jnp.where(lane == e1, before, 0.0), axis=-1, keepdims=True).astype(jnp.int32)
    carry = carry_ref[...] + jnp.sum(oh, axis=0, keepdims=True)
    carry_ref[...] = carry
    cnt_ref[...] = carry

    ri = jnp.where(lane == 0, e0, jnp.where(lane == 1, e1, jnp.where(lane == 2, rank0, rank1)))
    ri_ref[...] = ri[:, :SUBLANES]
    rw_ref[...] = jnp.where(lane == 0, w0, w1)[:, :SUBLANES]


def _merge(oda, omla, gates, x2, wo, fg, whi, wlo, br):
    T = x2.shape[0]
    tm = PROJ_ROWS
    row = lambda w: pl.BlockSpec((tm, w), lambda i: (i, 0))
    outs = [jax.ShapeDtypeStruct((T, D_MODEL), F32),
            jax.ShapeDtypeStruct((T * ROW_CHUNKS, LANES), F32),
            jax.ShapeDtypeStruct((T, SUBLANES), jnp.int32),
            jax.ShapeDtypeStruct((T, SUBLANES), F32),
            jax.ShapeDtypeStruct((1, LANES), F32)]
    out_specs = [row(D_MODEL), pl.BlockSpec((tm * ROW_CHUNKS, LANES), lambda i: (i, 0)),
                 row(SUBLANES), row(SUBLANES), pl.BlockSpec((1, LANES), lambda i: (0, 0))]
    return pl.pallas_call(
        _merge_kernel, out_shape=outs, grid=(T // tm,),
        in_specs=[row(D_MODEL), row(D_MODEL), row(2 * D_MODEL), row(D_MODEL), _const_spec(wo.shape),
                  _const_spec(fg.shape), _const_spec(whi.shape), _const_spec(wlo.shape),
                  _const_spec(br.shape)],
        out_specs=out_specs,
        scratch_shapes=[pltpu.VMEM((1, LANES), F32)],
        compiler_params=pltpu.CompilerParams(dimension_semantics=("arbitrary",),
                                             vmem_limit_bytes=VMEM_LIMIT),
        name="merge_router")(oda, omla, gates, x2, wo, fg, whi, wlo, br)


def _row(ref, r):
    return ref.at[pl.ds(pl.multiple_of(r * ROW_CHUNKS, ROW_CHUNKS), ROW_CHUNKS), :]


def _expert_kernel(ce_ref, nu_ref, tok_ref, tok_next_ref, h2_ref, w1_ref, w3_ref, w2_ref, yb_ref,
                   buf_ref, sem):
    c = pl.program_id(0)
    ch = MOE_CHUNK
    slot = c & 1
    n_used = nu_ref[0]

    def gather(toks, sl):
        def issue(t, carry):
            pltpu.make_async_copy(_row(h2_ref, toks[0, 0, t]), _row(buf_ref.at[sl], t),
                                  sem.at[sl]).start()
            return carry

        lax.fori_loop(0, ch, issue, 0, unroll=8)

    @pl.when(c == 0)
    def _():
        gather(tok_ref, 0)

    @pl.when(c + 1 < n_used)
    def _():
        gather(tok_next_ref, 1 - slot)

    @pl.when(c < n_used)
    def _():
        pltpu.make_async_copy(h2_ref.at[pl.ds(0, ch * ROW_CHUNKS), :], buf_ref.at[slot],
                              sem.at[slot]).wait()
        x = jnp.concatenate([buf_ref[slot, pl.ds(s, ch, stride=ROW_CHUNKS), :]
                             for s in range(ROW_CHUNKS)], axis=-1).astype(BF16)
        a = _dot(x, w1_ref[...])
        b = _dot(x, w3_ref[...])
        hmid = (a / (1.0 + jnp.exp(-a))) * b
        y = _dot(hmid.astype(BF16), w2_ref[...])
        for s in range(ROW_CHUNKS):
            yb_ref[pl.ds(s, ch, stride=ROW_CHUNKS), :] = y[:, s * LANES:(s + 1) * LANES]

    @pl.when(c >= n_used)
    def _():
        yb_ref[...] = jnp.zeros_like(yb_ref)


def _experts(chunk_e, n_used, tok3, h2r, w1, w3, w2):
    ch = MOE_CHUNK
    n_chunks = tok3.shape[0]
    smem = lambda f: pl.BlockSpec((1, 1, ch), f, memory_space=pltpu.SMEM)
    wspec = lambda s: pl.BlockSpec((None,) + s, lambda c, ce, nu: (ce[c], 0, 0))
    return pl.pallas_call(
        _expert_kernel,
        out_shape=jax.ShapeDtypeStruct((n_chunks * ch * ROW_CHUNKS, LANES), F32),
        grid_spec=pltpu.PrefetchScalarGridSpec(
            num_scalar_prefetch=2, grid=(n_chunks,),
            in_specs=[smem(lambda c, ce, nu: (c, 0, 0)),
                      smem(lambda c, ce, nu: (jnp.minimum(c + 1, n_chunks - 1), 0, 0)),
                      pl.BlockSpec(memory_space=pl.ANY),
                      wspec((D_MODEL, D_EXPERT)), wspec((D_MODEL, D_EXPERT)),
                      wspec((D_EXPERT, D_MODEL))],
            out_specs=pl.BlockSpec((ch * ROW_CHUNKS, LANES), lambda c, ce, nu: (c, 0)),
            scratch_shapes=[pltpu.VMEM((2, ch * ROW_CHUNKS, LANES), F32),
                            pltpu.SemaphoreType.DMA((2,))]),
        compiler_params=pltpu.CompilerParams(dimension_semantics=("arbitrary",),
                                             vmem_limit_bytes=VMEM_LIMIT),
        name="experts")(chunk_e, n_used, tok3, tok3, h2r, w1, w3, w2)


def _combine_kernel(dest_ref, yb_ref, x1_ref, rw_ref, o_ref, buf_ref, sem):
    tb = ROUTE_ROWS

    def issue(t, c):
        for k in range(TOP_K):
            pltpu.make_async_copy(_row(yb_ref, dest_ref[0, 0, 2 * t + k]),
                                  _row(buf_ref.at[k], t), sem).start()
        return c

    lax.fori_loop(0, tb, issue, 0)

    def drain(t, c):
        pltpu.make_async_copy(_row(yb_ref, 0), _row(buf_ref.at[0], 0), sem).wait()
        return c

    lax.fori_loop(0, TOP_K * tb, drain, 0)
    rw = rw_ref[...]
    acc = x1_ref[...]
    for k in range(TOP_K):
        y = jnp.concatenate([buf_ref[k, pl.ds(s, tb, stride=ROW_CHUNKS), :] for s in range(ROW_CHUNKS)],
                            axis=-1)
        acc = acc + y * rw[:, k:k + 1]
    o_ref[...] = acc


def _combine(dest3, yb, x1, rw):
    T = x1.shape[0]
    tb = ROUTE_ROWS
    return pl.pallas_call(
        _combine_kernel,
        out_shape=jax.ShapeDtypeStruct((T, D_MODEL), F32),
        grid=(T // tb,),
        in_specs=[pl.BlockSpec((1, 1, 2 * tb), lambda i: (i, 0, 0), memory_space=pltpu.SMEM),
                  pl.BlockSpec(memory_space=pl.ANY),
                  pl.BlockSpec((tb, D_MODEL), lambda i: (i, 0)),
                  pl.BlockSpec((tb, SUBLANES), lambda i: (i, 0))],
        out_specs=pl.BlockSpec((tb, D_MODEL), lambda i: (i, 0)),
        scratch_shapes=[pltpu.VMEM((TOP_K, tb * ROW_CHUNKS, LANES), F32), pltpu.SemaphoreType.DMA],
        compiler_params=pltpu.CompilerParams(dimension_semantics=("arbitrary",),
                                             vmem_limit_bytes=VMEM_LIMIT),
        name="combine")(dest3, yb, x1, rw)


def _rope_tables(positions, dim, half_lanes, one_fill):
    inv = ROPE_THETA ** (-jnp.arange(0, dim, 2, dtype=F32) / dim)
    ang = positions.astype(F32).reshape(-1, 1) * inv
    c, s = jnp.cos(ang), jnp.sin(ang)
    T = ang.shape[0]
    fill = half_lanes - dim
    cseg = jnp.concatenate([c, c, jnp.full((T, fill), one_fill, F32)], axis=-1)
    sseg = jnp.concatenate([-s, s, jnp.zeros((T, fill), F32)], axis=-1)
    reps = LANES // half_lanes
    return jnp.tile(cseg, (1, reps)), jnp.tile(sseg, (1, reps))


def _layer(x2, B, S, tables, lambda_init, attn_norm_g, w_in, b_gate, da_q_norm_g, da_k_norm_g,
           da_lambda_q1, da_lambda_k1, da_lambda_q2, da_lambda_k2, da_subln_g, mla_q_lora_g,
           mla_w_uq, mla_kv_lora_g, mla_w_ukv, mla_q_norm_g, mla_k_nope_norm_g, mla_k_rope_norm_g,
           w_o, ffn_norm_g, w_group, b_group, w_router, b_router, w1, w3, w2):
    T = B * S
    D = D_MODEL
    c0, c1, c2 = D, 2 * D, 3 * D
    c3 = c2 + MLA_Q_LORA
    c4 = c3 + MLA_KV_LORA
    c5 = c4 + MLA_ROPE
    wb = w_in.astype(BF16)
    wkr = jnp.pad(wb[:, c4:c5], ((0, 0), (0, LANES - MLA_ROPE)))
    qk = MLA_NOPE + MLA_ROPE
    uq = mla_w_uq.astype(BF16).reshape(MLA_Q_LORA, MLA_HEADS, qk)
    wuq = jnp.concatenate([
        uq[:, :, :MLA_NOPE].reshape(MLA_Q_LORA, D),
        jnp.pad(uq[:, :, MLA_NOPE:], ((0, 0), (0, 0), (0, LANES - MLA_ROPE))).reshape(MLA_Q_LORA, D)],
        axis=1)
    ukv = mla_w_ukv.astype(BF16).reshape(MLA_KV_LORA, MLA_HEADS, MLA_NOPE + MLA_V)
    wukv = jnp.concatenate([ukv[:, :, :MLA_NOPE].reshape(MLA_KV_LORA, D),
                            ukv[:, :, MLA_NOPE:].reshape(MLA_KV_LORA, D)], axis=1)
    wts = (wb[:, :c0], wb[:, c0:c1], wb[:, c1:c2], wb[:, c2:c3], wb[:, c3:c4], wkr, wb[:, c5:],
           wuq, wukv)
    r1 = lambda v: v.astype(F32).reshape(1, -1)
    pad_rope = lambda v: jnp.pad(v.astype(F32), (0, LANES - MLA_ROPE)).reshape(1, LANES)
    vecs = (r1(b_gate),
            jnp.tile(r1(da_q_norm_g), (1, 2)) * (DA_HEAD_DIM ** -0.5),
            jnp.tile(r1(da_k_norm_g), (1, 2)),
            r1(mla_q_lora_g), r1(mla_kv_lora_g),
            r1(mla_q_norm_g[:MLA_NOPE]) * (qk ** -0.5),
            pad_rope(mla_q_norm_g[MLA_NOPE:]) * (qk ** -0.5),
            r1(mla_k_nope_norm_g), pad_rope(mla_k_rope_norm_g))
    q, k, v, qn, qr, kn, vm, kr, gates = _in_proj(x2, r1(attn_norm_g), wts, vecs, tables)

    b3 = lambda a: a.reshape(B, S, a.shape[-1])
    lam_vecs = jnp.stack([da_lambda_q1, da_lambda_k1, da_lambda_q2, da_lambda_k2]).astype(F32)
    o_da = _da_attn(b3(q), b3(k), b3(v), lam_vecs, r1(da_subln_g), lambda_init)
    o_mla = _mla_attn(b3(qn), b3(qr), b3(kn), b3(kr), b3(vm))

    wgr = jnp.concatenate([w_group.astype(F32), w_router.astype(F32),
                           jnp.zeros((D, LANES - N_GROUPS - N_EXPERTS), F32)], axis=1)
    whi = wgr.astype(BF16)
    wlo = (wgr - whi.astype(F32)).astype(BF16)
    br = jnp.concatenate([b_group.astype(F32), b_router.astype(F32),
                          jnp.zeros((LANES - N_GROUPS - N_EXPERTS,), F32)]).reshape(1, LANES)
    x1, h2r, ri, rw, cnt = _merge(o_da.reshape(T, D), o_mla.reshape(T, D), gates, x2,
                                  w_o.astype(BF16), r1(ffn_norm_g), whi, wlo, br)

    ch = MOE_CHUNK
    counts = cnt[0, :N_EXPERTS].astype(jnp.int32)
    padded = (counts + ch - 1) // ch * ch
    pends = jnp.cumsum(padded)
    pstarts = pends - padded
    n_chunks = (T * TOP_K) // ch + N_EXPERTS
    P = n_chunks * ch
    dest = pstarts[ri[:, :TOP_K]] + ri[:, TOP_K:2 * TOP_K]
    dest3 = dest.reshape(T // ROUTE_ROWS, 1, TOP_K * ROUTE_ROWS)
    chunk_start = jnp.arange(n_chunks, dtype=jnp.int32)[:, None] * ch
    chunk_e = jnp.minimum(jnp.sum((pends[None, :] <= chunk_start).astype(jnp.int32), axis=1),
                          N_EXPERTS - 1).astype(jnp.int32)
    n_used = (pends[-1:] // ch).astype(jnp.int32)
    tok = jnp.repeat(jnp.arange(T, dtype=jnp.int32), TOP_K)
    tok3 = jnp.zeros((P,), jnp.int32).at[dest.reshape(-1)].set(tok).reshape(n_chunks, 1, ch)

    yb = _experts(chunk_e, n_used, tok3, h2r, w1.astype(BF16), w3.astype(BF16), w2.astype(BF16))
    return _combine(dest3, yb, x1, rw)


def kernel(x, positions, attn_norm_g, w_in, b_gate, da_q_norm_g, da_k_norm_g, da_lambda_q1, da_lambda_k1, da_lambda_q2, da_lambda_k2, da_subln_g, mla_q_lora_g, mla_w_uq, mla_kv_lora_g, mla_w_ukv, mla_q_norm_g, mla_k_nope_norm_g, mla_k_rope_norm_g, w_o, ffn_norm_g, w_group, b_group, w_router, b_router, w1, w3, w2):
    B, S, D = x.shape
    assert D == D_MODEL and S % MLA_TQ == 0 and (B * S) % PROJ_ROWS == 0
    tables = (_rope_tables(positions, DA_ROT_DIM, DA_HEAD_DIM, 1.0)
              + _rope_tables(positions, MLA_ROPE, LANES, 1.0))
    x2 = x.reshape(B * S, D)
    per_layer = (attn_norm_g, w_in, b_gate, da_q_norm_g, da_k_norm_g, da_lambda_q1, da_lambda_k1,
                 da_lambda_q2, da_lambda_k2, da_subln_g, mla_q_lora_g, mla_w_uq, mla_kv_lora_g,
                 mla_w_ukv, mla_q_norm_g, mla_k_nope_norm_g, mla_k_rope_norm_g, w_o, ffn_norm_g,
                 w_group, b_group, w_router, b_router, w1, w3, w2)
    for l in range(w_in.shape[0]):
        lambda_init = 0.8 - 0.6 * math.exp(-0.3 * l)
        x2 = _layer(x2, B, S, tables, lambda_init, *[p[l] for p in per_layer])
    return x2.reshape(B, S, D)
```

```python
import functools
import math

import jax
import jax.numpy as jnp
from jax import lax
from jax.experimental import pallas as pl
from jax.experimental.pallas import tpu as pltpu

D_MODEL = 1024
DA_HEADS = 8
DA_HEAD_DIM = 64
DA_V_DIM = 128
DA_ROT_DIM = 16
MLA_HEADS = 8
MLA_Q_LORA = 384
MLA_KV_LORA = 256
MLA_NOPE = 128
MLA_ROPE = 64
MLA_V = 128
ROPE_THETA = 500000.0
N_GROUPS = 4
EXPERTS_PER_GROUP = 8
N_EXPERTS = N_GROUPS * EXPERTS_PER_GROUP
TOP_K = 2
D_EXPERT = 512
NORM_EPS = 1e-6
LOG2_E = math.log2(math.e)

LANES = 128
SUBLANES = 8
ROW_CHUNKS = D_MODEL // LANES

PROJ_ROWS = 256
DA_TQ = 256
MLA_TQ = 512
ATT_TK = 512
MOE_CHUNK = 256
ROUTE_ROWS = 256
VMEM_LIMIT = 48 * 1024 * 1024

F32 = jnp.float32
BF16 = jnp.bfloat16


def _const_spec(shape):
    nd = len(shape)
    return pl.BlockSpec(shape, lambda *_: (0,) * nd, pipeline_mode=pl.Buffered(1))


def _rms(v, n):
    return lax.rsqrt(jnp.sum(v * v, axis=-1, keepdims=True) * (1.0 / n) + NORM_EPS)


def _dot(a, b):
    return jnp.dot(a, b, preferred_element_type=F32)


def _dot_nt(a, b):
    return lax.dot_general(a, b, (((1,), (1,)), ((), ())), preferred_element_type=F32)


def _in_proj_kernel(x_ref, g_ref, wq_ref, wk_ref, wv_ref, wcq_ref, wckv_ref, wkr_ref, wg_ref,
                    wuq_ref, wukv_ref, bg_ref, qg_ref, kg_ref, qlg_ref, kvlg_ref, mqn_ref, mqr_ref,
                    mkn_ref, mkr_ref, cda_ref, sda_ref, cm_ref, sm_ref,
                    q_ref, k_ref, v_ref, qn_ref, qr_ref, kn_ref, vm_ref, kr_ref, gate_ref):
    x = x_ref[...]
    h = (x * _rms(x, D_MODEL) * g_ref[...]).astype(BF16)
    rows = x.shape[0]
    lane = lax.broadcasted_iota(jnp.int32, (rows, LANES), 1)
    lo = lane < DA_HEAD_DIM
    da_first = (lane & (DA_HEAD_DIM - 1)) < (DA_ROT_DIM // 2)
    m_first = lane < (MLA_ROPE // 2)
    cda, sda = cda_ref[...], sda_ref[...]
    cm, sm = cm_ref[...], sm_ref[...]

    def da_rope(y):
        partner = jnp.where(da_first, pltpu.roll(y, LANES - DA_ROT_DIM // 2, 1),
                            pltpu.roll(y, DA_ROT_DIM // 2, 1))
        return y * cda + partner * sda

    def mla_rope(y):
        partner = jnp.where(m_first, pltpu.roll(y, LANES - MLA_ROPE // 2, 1),
                            pltpu.roll(y, MLA_ROPE // 2, 1))
        return y * cm + partner * sm

    for w_ref, gain_ref, o_ref in ((wq_ref, qg_ref, q_ref), (wk_ref, kg_ref, k_ref)):
        p_all = _dot(h, w_ref[...])
        gain = gain_ref[...]
        for hd in range(DA_HEADS):
            sl = slice(hd * LANES, (hd + 1) * LANES)
            p = p_all[:, sl]
            p2 = p * p
            s_all = jnp.sum(p2, axis=-1, keepdims=True)
            s_lo = jnp.sum(jnp.where(lo, p2, 0.0), axis=-1, keepdims=True)
            inv = jnp.where(lo, lax.rsqrt(s_lo * (1.0 / DA_HEAD_DIM) + NORM_EPS),
                            lax.rsqrt((s_all - s_lo) * (1.0 / DA_HEAD_DIM) + NORM_EPS))
            o_ref[:, sl] = da_rope(p * inv * gain).astype(BF16)
    v_ref[...] = _dot(h, wv_ref[...]).astype(BF16)

    cq = _dot(h, wcq_ref[...])
    cqn = (cq * _rms(cq, MLA_Q_LORA) * qlg_ref[...]).astype(BF16)
    qm = _dot(cqn, wuq_ref[...])
    gn, gr = mqn_ref[...], mqr_ref[...]
    for hd in range(MLA_HEADS):
        sl = slice(hd * LANES, (hd + 1) * LANES)
        qn = qm[:, sl]
        qr = qm[:, D_MODEL + hd * LANES:D_MODEL + (hd + 1) * LANES]
        ss = jnp.sum(qn * qn, axis=-1, keepdims=True) + jnp.sum(qr * qr, axis=-1, keepdims=True)
        inv = lax.rsqrt(ss * (1.0 / (MLA_NOPE + MLA_ROPE)) + NORM_EPS)
        qn_ref[:, sl] = (qn * inv * gn).astype(BF16)
        qr_ref[:, sl] = mla_rope(qr * inv * gr).astype(BF16)

    ckv = _dot(h, wckv_ref[...])
    ckvn = (ckv * _rms(ckv, MLA_KV_LORA) * kvlg_ref[...]).astype(BF16)
    kv = _dot(ckvn, wukv_ref[...])
    gkn = mkn_ref[...]
    for hd in range(MLA_HEADS):
        sl = slice(hd * LANES, (hd + 1) * LANES)
        kn = kv[:, sl]
        kn_ref[:, sl] = (kn * _rms(kn, MLA_NOPE) * gkn).astype(BF16)
    vm_ref[...] = kv[:, D_MODEL:].astype(BF16)
    kr = _dot(h, wkr_ref[...])
    kr_ref[...] = mla_rope(kr * _rms(kr, MLA_ROPE) * mkr_ref[...]).astype(BF16)

    gl = _dot(h, wg_ref[...]) + bg_ref[...]
    gate_ref[...] = (1.0 / (1.0 + jnp.exp(-gl))).astype(BF16)


def _in_proj(x2, g, wts, vecs, tables):
    T = x2.shape[0]
    tm = PROJ_ROWS
    row = lambda w: pl.BlockSpec((tm, w), lambda i: (i, 0))
    in_specs = ([row(D_MODEL), _const_spec(g.shape)] + [_const_spec(w.shape) for w in wts]
                + [_const_spec(v.shape) for v in vecs] + [row(LANES)] * 4)
    outs = [jax.ShapeDtypeStruct((T, D_MODEL), BF16)] * 7 + [
        jax.ShapeDtypeStruct((T, LANES), BF16), jax.ShapeDtypeStruct((T, 2 * D_MODEL), BF16)]
    out_specs = [row(D_MODEL)] * 7 + [row(LANES), row(2 * D_MODEL)]
    return pl.pallas_call(
        _in_proj_kernel, out_shape=outs, grid=(T // tm,), in_specs=in_specs, out_specs=out_specs,
        compiler_params=pltpu.CompilerParams(dimension_semantics=("parallel",),
                                             vmem_limit_bytes=VMEM_LIMIT),
        name="in_proj")(x2, g, *wts, *vecs, *tables)


def _flash(qs, load_k, load_v, qi, tq, n_maps):
    rows = n_maps * tq
    tk = ATT_TK

    def scores(j):
        return _dot_nt(qs, load_k(j))

    def update(j, s, carry, masked):
        m, l, acc = carry
        if masked:
            qpos = qi * tq + (lax.broadcasted_iota(jnp.int32, (rows, tk), 0) & (tq - 1))
            kpos = j * tk + lax.broadcasted_iota(jnp.int32, (rows, tk), 1)
            s = jnp.where(kpos <= qpos, s, -jnp.inf)
        m_new = jnp.maximum(m, jnp.max(s, axis=-1, keepdims=True))
        alpha = jnp.exp2(m - m_new)
        p = jnp.exp2(s - m_new)
        psum = p[:, :LANES]
        for c in range(1, tk // LANES):
            psum = psum + p[:, c * LANES:(c + 1) * LANES]
        l = alpha * l + psum
        acc = alpha * acc + _dot(p.astype(BF16), load_v(j))
        return m_new, l, acc

    init = (jnp.full((rows, 1), -jnp.inf, F32), jnp.zeros((rows, LANES), F32),
            jnp.zeros((rows, LANES), F32))
    n_full = (qi * tq) // tk
    n_diag = max(1, tq // tk)

    def body(j, carry):
        m, l, acc, s = carry
        s_next = scores(j + 1)
        return update(j, s, (m, l, acc), False) + (s_next,)

    m, l, acc, s = lax.fori_loop(0, n_full, body, init + (scores(0),))
    for d in range(n_diag):
        s_next = scores(n_full + d + 1) if d + 1 < n_diag else None
        m, l, acc = update(n_full + d, s, (m, l, acc), True)
        s = s_next
    return acc, jnp.sum(l, axis=-1, keepdims=True)


def _kv_block(j):
    return pl.ds(pl.multiple_of(j * ATT_TK, ATT_TK), ATT_TK)


def _da_attn_kernel(q_ref, k_ref, v_ref, lam_ref, sg_ref, o_ref, *, lambda_init):
    qi = pl.program_id(2)
    tq = DA_TQ
    q = q_ref[...]
    lane = lax.broadcasted_iota(jnp.int32, q.shape, 1)
    zero = jnp.zeros_like(q)
    qs = jnp.concatenate([jnp.where(lane < DA_HEAD_DIM, q, zero),
                          jnp.where(lane >= DA_HEAD_DIM, q, zero)], axis=0)

    acc, l = _flash(qs, lambda j: k_ref[_kv_block(j), :], lambda j: v_ref[_kv_block(j), :],
                    qi, tq, 2)
    o1 = acc[:tq] / l[:tq]
    o2 = acc[tq:] / l[tq:]
    lv = lam_ref[...]
    lam = (jnp.exp(jnp.sum(lv[0:1] * lv[1:2], axis=-1, keepdims=True))
           - jnp.exp(jnp.sum(lv[2:3] * lv[3:4], axis=-1, keepdims=True)) + lambda_init)
    o = o1 - lam * o2
    y = o * _rms(o, DA_V_DIM) * sg_ref[...] * (1.0 - lambda_init)
    o_ref[...] = y.astype(BF16)


def _da_attn(q, k, v, lam_vecs, subln_g, lambda_init):
    B, S, _ = q.shape
    tq = DA_TQ
    qspec = pl.BlockSpec((None, tq, LANES), lambda b, h, i: (b, i, h))
    kvspec = pl.BlockSpec((None, S, LANES), lambda b, h, i: (b, 0, h))
    return pl.pallas_call(
        functools.partial(_da_attn_kernel, lambda_init=lambda_init),
        out_shape=jax.ShapeDtypeStruct((B, S, D_MODEL), BF16),
        grid=(B, DA_HEADS, S // tq),
        in_specs=[qspec, kvspec, kvspec, _const_spec(lam_vecs.shape), _const_spec(subln_g.shape)],
        out_specs=qspec,
        compiler_params=pltpu.CompilerParams(
            dimension_semantics=("parallel", "parallel", "arbitrary"), vmem_limit_bytes=VMEM_LIMIT),
        name="da_attn")(q, k, v, lam_vecs, subln_g)


def _mla_attn_kernel(qn_ref, qr_ref, kn_ref, kr_ref, v_ref, o_ref):
    qi = pl.program_id(2)
    qs = jnp.concatenate([qn_ref[...], qr_ref[...]], axis=-1)

    def load_k(j):
        return jnp.concatenate([kn_ref[_kv_block(j), :], kr_ref[_kv_block(j), :]], axis=-1)

    acc, l = _flash(qs, load_k, lambda j: v_ref[_kv_block(j), :], qi, MLA_TQ, 1)
    o_ref[...] = (acc / l).astype(BF16)


def _mla_attn(qn, qr, kn, kr, vm):
    B, S, _ = qn.shape
    tq = MLA_TQ
    qspec = pl.BlockSpec((None, tq, LANES), lambda b, h, i: (b, i, h))
    kvspec = pl.BlockSpec((None, S, LANES), lambda b, h, i: (b, 0, h))
    krspec = pl.BlockSpec((None, S, LANES), lambda b, h, i: (b, 0, 0))
    return pl.pallas_call(
        _mla_attn_kernel,
        out_shape=jax.ShapeDtypeStruct((B, S, D_MODEL), BF16),
        grid=(B, MLA_HEADS, S // tq),
        in_specs=[qspec, qspec, kvspec, krspec, kvspec],
        out_specs=qspec,
        compiler_params=pltpu.CompilerParams(
            dimension_semantics=("parallel", "parallel", "arbitrary"), vmem_limit_bytes=VMEM_LIMIT),
        name="mla_attn")(qn, qr, kn, kr, vm)


def _merge_kernel(oda_ref, omla_ref, gate_ref, x_ref, wo_ref, fg_ref, whi_ref, wlo_ref, br_ref,
                  x1_ref, h2_ref, ri_ref, rw_ref, cnt_ref, carry_ref):
    i = pl.program_id(0)
    rows = x_ref.shape[0]

    @pl.when(i == 0)
    def _():
        carry_ref[...] = jnp.zeros_like(carry_ref)

    g = gate_ref[...].astype(F32)
    mixed = g[:, :D_MODEL] * oda_ref[...].astype(F32) + g[:, D_MODEL:] * omla_ref[...].astype(F32)
    x1 = x_ref[...] + _dot(mixed.astype(BF16), wo_ref[...])
    x1_ref[...] = x1
    h2 = x1 * _rms(x1, D_MODEL) * fg_ref[...]
    for s in range(ROW_CHUNKS):
        h2_ref[pl.ds(s, rows, stride=ROW_CHUNKS), :] = h2[:, s * LANES:(s + 1) * LANES]

    hi = h2.astype(BF16)
    lo = (h2 - hi.astype(F32)).astype(BF16)
    whi = whi_ref[...]
    logits = _dot(hi, whi) + _dot(hi, wlo_ref[...]) + _dot(lo, whi) + br_ref[...]

    lane = lax.broadcasted_iota(jnp.int32, (rows, LANES), 1)
    neg = -jnp.inf
    gl = jnp.where(lane < N_GROUPS, logits, neg)
    gmax = jnp.max(gl, axis=-1, keepdims=True)
    g_sel = jnp.min(jnp.where(gl == gmax, lane, LANES), axis=-1, keepdims=True)
    p_g = 1.0 / jnp.sum(jnp.exp(gl - gmax), axis=-1, keepdims=True)
    e_lo = N_GROUPS + g_sel * EXPERTS_PER_GROUP
    el = jnp.where((lane >= e_lo) & (lane < e_lo + EXPERTS_PER_GROUP), logits, neg)
    v0 = jnp.max(el, axis=-1, keepdims=True)
    i0 = jnp.min(jnp.where(el == v0, lane, LANES), axis=-1, keepdims=True)
    el2 = jnp.where(lane == i0, neg, el)
    v1 = jnp.max(el2, axis=-1, keepdims=True)
    i1 = jnp.min(jnp.where(el2 == v1, lane, LANES), axis=-1, keepdims=True)
    t = jnp.exp(v1 - v0)
    w0 = p_g / (1.0 + t)
    w1 = p_g * t / (1.0 + t)
    e0 = i0 - N_GROUPS
    e1 = i1 - N_GROUPS

    oh = jnp.where((lane == e0) | (lane == e1), 1.0, 0.0)
    r_i = lax.broadcasted_iota(jnp.int32, (rows, rows), 0)
    c_i = lax.broadcasted_iota(jnp.int32, (rows, rows), 1)
    tri = jnp.where(c_i < r_i, 1.0, 0.0).astype(BF16)
    before = _dot(tri, oh.astype(BF16)) + carry_ref[...]
    rank0 = jnp.sum(jnp.where(lane == e0, before, 0.0), axis=-1, keepdims=True).astype(jnp.int32)
    rank1 = jnp.sum(jnp.where(lane == e1, before, 0.0), axis=-1, keepdims=True).astype(jnp.int32)
    carry = carry_ref[...] + jnp.sum(oh, axis=0, keepdims=True)
    carry_ref[...] = carry
    cnt_ref[...] = carry

    ri = jnp.where(lane == 0, e0, jnp.where(lane == 1, e1, jnp.where(lane == 2, rank0, rank1)))
    ri_ref[...] = ri[:, :SUBLANES]
    rw_ref[...] = jnp.where(lane == 0, w0, w1)[:, :SUBLANES]


def _merge(oda, omla, gates, x2, wo, fg, whi, wlo, br):
    T = x2.shape[0]
    tm = PROJ_ROWS
    row = lambda w: pl.BlockSpec((tm, w), lambda i: (i, 0))
    outs = [jax.ShapeDtypeStruct((T, D_MODEL), F32),
            jax.ShapeDtypeStruct((T * ROW_CHUNKS, LANES), F32),
            jax.ShapeDtypeStruct((T, SUBLANES), jnp.int32),
            jax.ShapeDtypeStruct((T, SUBLANES), F32),
            jax.ShapeDtypeStruct((1, LANES), F32)]
    out_specs = [row(D_MODEL), pl.BlockSpec((tm * ROW_CHUNKS, LANES), lambda i: (i, 0)),
                 row(SUBLANES), row(SUBLANES), pl.BlockSpec((1, LANES), lambda i: (0, 0))]
    return pl.pallas_call(
        _merge_kernel, out_shape=outs, grid=(T // tm,),
        in_specs=[row(D_MODEL), row(D_MODEL), row(2 * D_MODEL), row(D_MODEL), _const_spec(wo.shape),
                  _const_spec(fg.shape), _const_spec(whi.shape), _const_spec(wlo.shape),
                  _const_spec(br.shape)],
        out_specs=out_specs,
        scratch_shapes=[pltpu.VMEM((1, LANES), F32)],
        compiler_params=pltpu.CompilerParams(dimension_semantics=("arbitrary",),
                                             vmem_limit_bytes=VMEM_LIMIT),
        name="merge_router")(oda, omla, gates, x2, wo, fg, whi, wlo, br)


def _row(ref, r):
    return ref.at[pl.ds(pl.multiple_of(r * ROW_CHUNKS, ROW_CHUNKS), ROW_CHUNKS), :]


def _expert_kernel(ce_ref, nu_ref, tok_ref, tok_next_ref, h2_ref, w1_ref, w3_ref, w2_ref, yb_ref,
                   buf_ref, sem):
    c = pl.program_id(0)
    ch = MOE_CHUNK
    slot = c & 1
    n_used = nu_ref[0]

    def gather(toks, sl):
        def issue(t, carry):
            pltpu.make_async_copy(_row(h2_ref, toks[0, 0, t]), _row(buf_ref.at[sl], t),
                                  sem.at[sl]).start()
            return carry

        lax.fori_loop(0, ch, issue, 0, unroll=8)

    @pl.when(c == 0)
    def _():
        gather(tok_ref, 0)

    @pl.when(c + 1 < n_used)
    def _():
        gather(tok_next_ref, 1 - slot)

    @pl.when(c < n_used)
    def _():
        pltpu.make_async_copy(h2_ref.at[pl.ds(0, ch * ROW_CHUNKS), :], buf_ref.at[slot],
                              sem.at[slot]).wait()
        x = jnp.concatenate([buf_ref[slot, pl.ds(s, ch, stride=ROW_CHUNKS), :]
                             for s in range(ROW_CHUNKS)], axis=-1).astype(BF16)
        a = _dot(x, w1_ref[...])
        b = _dot(x, w3_ref[...])
        hmid = (a / (1.0 + jnp.exp(-a))) * b
        y = _dot(hmid.astype(BF16), w2_ref[...])
        for s in range(ROW_CHUNKS):
            yb_ref[pl.ds(s, ch, stride=ROW_CHUNKS), :] = y[:, s * LANES:(s + 1) * LANES]

    @pl.when(c >= n_used)
    def _():
        yb_ref[...] = jnp.zeros_like(yb_ref)


def _experts(chunk_e, n_used, tok3, h2r, w1, w3, w2):
    ch = MOE_CHUNK
    n_chunks = tok3.shape[0]
    smem = lambda f: pl.BlockSpec((1, 1, ch), f, memory_space=pltpu.SMEM)
    wspec = lambda s: pl.BlockSpec((None,) + s, lambda c, ce, nu: (ce[c], 0, 0))
    return pl.pallas_call(
        _expert_kernel,
        out_shape=jax.ShapeDtypeStruct((n_chunks * ch * ROW_CHUNKS, LANES), F32),
        grid_spec=pltpu.PrefetchScalarGridSpec(
            num_scalar_prefetch=2, grid=(n_chunks,),
            in_specs=[smem(lambda c, ce, nu: (c, 0, 0)),
                      smem(lambda c, ce, nu: (jnp.minimum(c + 1, n_chunks - 1), 0, 0)),
                      pl.BlockSpec(memory_space=pl.ANY),
                      wspec((D_MODEL, D_EXPERT)), wspec((D_MODEL, D_EXPERT)),
                      wspec((D_EXPERT, D_MODEL))],
            out_specs=pl.BlockSpec((ch * ROW_CHUNKS, LANES), lambda c, ce, nu: (c, 0)),
            scratch_shapes=[pltpu.VMEM((2, ch * ROW_CHUNKS, LANES), F32),
                            pltpu.SemaphoreType.DMA((2,))]),
        compiler_params=pltpu.CompilerParams(dimension_semantics=("arbitrary",),
                                             vmem_limit_bytes=VMEM_LIMIT),
        name="experts")(chunk_e, n_used, tok3, tok3, h2r, w1, w3, w2)


def _combine_kernel(dest_ref, yb_ref, x1_ref, rw_ref, o_ref, buf_ref, sem):
    tb = ROUTE_ROWS

    def issue(t, c):
        for k in range(TOP_K):
            pltpu.make_async_copy(_row(yb_ref, dest_ref[0, 0, 2 * t + k]),
                                  _row(buf_ref.at[k], t), sem).start()
        return c

    lax.fori_loop(0, tb, issue, 0)

    def drain(t, c):
        pltpu.make_async_copy(_row(yb_ref, 0), _row(buf_ref.at[0], 0), sem).wait()
        return c

    lax.fori_loop(0, TOP_K * tb, drain, 0)
    rw = rw_ref[...]
    acc = x1_ref[...]
    for k in range(TOP_K):
        y = jnp.concatenate([buf_ref[k, pl.ds(s, tb, stride=ROW_CHUNKS), :] for s in range(ROW_CHUNKS)],
                            axis=-1)
        acc = acc + y * rw[:, k:k + 1]
    o_ref[...] = acc


def _combine(dest3, yb, x1, rw):
    T = x1.shape[0]
    tb = ROUTE_ROWS
    return pl.pallas_call(
        _combine_kernel,
        out_shape=jax.ShapeDtypeStruct((T, D_MODEL), F32),
        grid=(T // tb,),
        in_specs=[pl.BlockSpec((1, 1, 2 * tb), lambda i: (i, 0, 0), memory_space=pltpu.SMEM),
                  pl.BlockSpec(memory_space=pl.ANY),
                  pl.BlockSpec((tb, D_MODEL), lambda i: (i, 0)),
                  pl.BlockSpec((tb, SUBLANES), lambda i: (i, 0))],
        out_specs=pl.BlockSpec((tb, D_MODEL), lambda i: (i, 0)),
        scratch_shapes=[pltpu.VMEM((TOP_K, tb * ROW_CHUNKS, LANES), F32), pltpu.SemaphoreType.DMA],
        compiler_params=pltpu.CompilerParams(dimension_semantics=("arbitrary",),
                                             vmem_limit_bytes=VMEM_LIMIT),
        name="combine")(dest3, yb, x1, rw)


def _rope_tables(positions, dim, half_lanes, one_fill):
    inv = ROPE_THETA ** (-jnp.arange(0, dim, 2, dtype=F32) / dim)
    ang = positions.astype(F32).reshape(-1, 1) * inv
    c, s = jnp.cos(ang), jnp.sin(ang)
    T = ang.shape[0]
    fill = half_lanes - dim
    cseg = jnp.concatenate([c, c, jnp.full((T, fill), one_fill, F32)], axis=-1)
    sseg = jnp.concatenate([-s, s, jnp.zeros((T, fill), F32)], axis=-1)
    reps = LANES // half_lanes
    return jnp.tile(cseg, (1, reps)), jnp.tile(sseg, (1, reps))


def _layer(x2, B, S, tables, lambda_init, attn_norm_g, w_in, b_gate, da_q_norm_g, da_k_norm_g,
           da_lambda_q1, da_lambda_k1, da_lambda_q2, da_lambda_k2, da_subln_g, mla_q_lora_g,
           mla_w_uq, mla_kv_lora_g, mla_w_ukv, mla_q_norm_g, mla_k_nope_norm_g, mla_k_rope_norm_g,
           w_o, ffn_norm_g, w_group, b_group, w_router, b_router, w1, w3, w2):
    T = B * S
    D = D_MODEL
    c0, c1, c2 = D, 2 * D, 3 * D
    c3 = c2 + MLA_Q_LORA
    c4 = c3 + MLA_KV_LORA
    c5 = c4 + MLA_ROPE
    wb = w_in.astype(BF16)
    wkr = jnp.pad(wb[:, c4:c5], ((0, 0), (0, LANES - MLA_ROPE)))
    qk = MLA_NOPE + MLA_ROPE
    uq = mla_w_uq.astype(BF16).reshape(MLA_Q_LORA, MLA_HEADS, qk)
    wuq = jnp.concatenate([
        uq[:, :, :MLA_NOPE].reshape(MLA_Q_LORA, D),
        jnp.pad(uq[:, :, MLA_NOPE:], ((0, 0), (0, 0), (0, LANES - MLA_ROPE))).reshape(MLA_Q_LORA, D)],
        axis=1)
    ukv = mla_w_ukv.astype(BF16).reshape(MLA_KV_LORA, MLA_HEADS, MLA_NOPE + MLA_V)
    wukv = jnp.concatenate([ukv[:, :, :MLA_NOPE].reshape(MLA_KV_LORA, D),
                            ukv[:, :, MLA_NOPE:].reshape(MLA_KV_LORA, D)], axis=1)
    wts = (wb[:, :c0], wb[:, c0:c1], wb[:, c1:c2], wb[:, c2:c3], wb[:, c3:c4], wkr, wb[:, c5:],
           wuq, wukv)
    r1 = lambda v: v.astype(F32).reshape(1, -1)
    pad_rope = lambda v: jnp.pad(v.astype(F32), (0, LANES - MLA_ROPE)).reshape(1, LANES)
    da_scale = DA_HEAD_DIM ** -0.5 * LOG2_E
    mla_scale = qk ** -0.5 * LOG2_E
    vecs = (r1(b_gate),
            jnp.tile(r1(da_q_norm_g), (1, 2)) * da_scale,
            jnp.tile(r1(da_k_norm_g), (1, 2)),
            r1(mla_q_lora_g), r1(mla_kv_lora_g),
            r1(mla_q_norm_g[:MLA_NOPE]) * mla_scale,
            pad_rope(mla_q_norm_g[MLA_NOPE:]) * mla_scale,
            r1(mla_k_nope_norm_g), pad_rope(mla_k_rope_norm_g))
    q, k, v, qn, qr, kn, vm, kr, gates = _in_proj(x2, r1(attn_norm_g), wts, vecs, tables)

    b3 = lambda a: a.reshape(B, S, a.shape[-1])
    lam_vecs = jnp.stack([da_lambda_q1, da_lambda_k1, da_lambda_q2, da_lambda_k2]).astype(F32)
    o_da = _da_attn(b3(q), b3(k), b3(v), lam_vecs, r1(da_subln_g), lambda_init)
    o_mla = _mla_attn(b3(qn), b3(qr), b3(kn), b3(kr), b3(vm))

    wgr = jnp.concatenate([w_group.astype(F32), w_router.astype(F32),
                           jnp.zeros((D, LANES - N_GROUPS - N_EXPERTS), F32)], axis=1)
    whi = wgr.astype(BF16)
    wlo = (wgr - whi.astype(F32)).astype(BF16)
    br = jnp.concatenate([b_group.astype(F32), b_router.astype(F32),
                          jnp.zeros((LANES - N_GROUPS - N_EXPERTS,), F32)]).reshape(1, LANES)
    x1, h2r, ri, rw, cnt = _merge(o_da.reshape(T, D), o_mla.reshape(T, D), gates, x2,
                                  w_o.astype(BF16), r1(ffn_norm_g), whi, wlo, br)

    ch = MOE_CHUNK
    counts = cnt[0, :N_EXPERTS].astype(jnp.int32)
    padded = (counts + ch - 1) // ch * ch
    pends = jnp.cumsum(padded)
    pstarts = pends - padded
    n_chunks = (T * TOP_K) // ch + N_EXPERTS
    P = n_chunks * ch
    dest = pstarts[ri[:, :TOP_K]] + ri[:, TOP_K:2 * TOP_K]
    dest3 = dest.reshape(T // ROUTE_ROWS, 1, TOP_K * ROUTE_ROWS)
    chunk_start = jnp.arange(n_chunks, dtype=jnp.int32)[:, None] * ch
    chunk_e = jnp.minimum(jnp.sum((pends[None, :] <= chunk_start).astype(jnp.int32), axis=1),
                          N_EXPERTS - 1).astype(jnp.int32)
    n_used = (pends[-1:] // ch).astype(jnp.int32)
    tok = jnp.repeat(jnp.arange(T, dtype=jnp.int32), TOP_K)
    tok3 = jnp.zeros((P,), jnp.int32).at[dest.reshape(-1)].set(tok).reshape(n_chunks, 1, ch)

    yb = _experts(chunk_e, n_used, tok3, h2r, w1.astype(BF16), w3.astype(BF16), w2.astype(BF16))
    return _combine(dest3, yb, x1, rw)


def kernel(x, positions, attn_norm_g, w_in, b_gate, da_q_norm_g, da_k_norm_g, da_lambda_q1, da_lambda_k1, da_lambda_q2, da_lambda_k2, da_subln_g, mla_q_lora_g, mla_w_uq, mla_kv_lora_g, mla_w_ukv, mla_q_norm_g, mla_k_nope_norm_g, mla_k_rope_norm_g, w_o, ffn_norm_g, w_group, b_group, w_router, b_router, w1, w3, w2):
    B, S, D = x.shape
    assert D == D_MODEL and S % MLA_TQ == 0 and (B * S) % PROJ_ROWS == 0
    tables = (_rope_tables(positions, DA_ROT_DIM, DA_HEAD_DIM, 1.0)
              + _rope_tables(positions, MLA_ROPE, LANES, 1.0))
    x2 = x.reshape(B * S, D)
    per_layer = (attn_norm_g, w_in, b_gate, da_q_norm_g, da_k_norm_g, da_lambda_q1, da_lambda_k1,
                 da_lambda_q2, da_lambda_k2, da_subln_g, mla_q_lora_g, mla_w_uq, mla_kv_lora_g,
                 mla_w_ukv, mla_q_norm_g, mla_k_nope_norm_g, mla_k_rope_norm_g, w_o, ffn_norm_g,
                 w_group, b_group, w_router, b_router, w1, w3, w2)
    for l in range(w_in.shape[0]):
        lambda_init = 0.8 - 0.6 * math.exp(-0.3 * l)
        x2 = _layer(x2, B, S, tables, lambda_init, *[p[l] for p in per_layer])
    return x2.reshape(B, S, D)
```

```python
import functools
import math

import jax
import jax.numpy as jnp
from jax import lax
from jax.experimental import pallas as pl
from jax.experimental.pallas import tpu as pltpu

D_MODEL = 1024
DA_HEADS = 8
DA_HEAD_DIM = 64
DA_V_DIM = 128
DA_ROT_DIM = 16
MLA_HEADS = 8
MLA_Q_LORA = 384
MLA_KV_LORA = 256
MLA_NOPE = 128
MLA_ROPE = 64
MLA_V = 128
ROPE_THETA = 500000.0
N_GROUPS = 4
EXPERTS_PER_GROUP = 8
N_EXPERTS = N_GROUPS * EXPERTS_PER_GROUP
TOP_K = 2
D_EXPERT = 512
NORM_EPS = 1e-6
LOG2_E = math.log2(math.e)

LANES = 128
SUBLANES = 8
ROW_CHUNKS = D_MODEL // LANES

PROJ_ROWS = 256
DA_TQ = 256
MLA_TQ = 512
ATT_TK = 512
MOE_CHUNK = 256
ROUTE_ROWS = 256
VMEM_LIMIT = 48 * 1024 * 1024

F32 = jnp.float32
BF16 = jnp.bfloat16


def _const_spec(shape):
    nd = len(shape)
    return pl.BlockSpec(shape, lambda *_: (0,) * nd, pipeline_mode=pl.Buffered(1))


def _rms(v, n):
    return lax.rsqrt(jnp.sum(v * v, axis=-1, keepdims=True) * (1.0 / n) + NORM_EPS)


def _dot(a, b):
    return jnp.dot(a, b, preferred_element_type=F32)


def _dot_nt(a, b):
    return lax.dot_general(a, b, (((1,), (1,)), ((), ())), preferred_element_type=F32)


def _in_proj_kernel(x_ref, g_ref, wq_ref, wk_ref, wv_ref, wcq_ref, wckv_ref, wkr_ref, wg_ref,
                    wuq_ref, wukv_ref, bg_ref, qg_ref, kg_ref, qlg_ref, kvlg_ref, mqn_ref, mqr_ref,
                    mkn_ref, mkr_ref, cda_ref, sda_ref, cm_ref, sm_ref,
                    q_ref, k_ref, v_ref, qn_ref, qr_ref, kn_ref, vm_ref, kr_ref, gate_ref):
    x = x_ref[...]
    h = (x * _rms(x, D_MODEL) * g_ref[...]).astype(BF16)
    rows = x.shape[0]
    lane = lax.broadcasted_iota(jnp.int32, (rows, LANES), 1)
    lo = lane < DA_HEAD_DIM
    da_first = (lane & (DA_HEAD_DIM - 1)) < (DA_ROT_DIM // 2)
    m_first = lane < (MLA_ROPE // 2)
    cda, sda = cda_ref[...], sda_ref[...]
    cm, sm = cm_ref[...], sm_ref[...]

    def da_rope(y):
        partner = jnp.where(da_first, pltpu.roll(y, LANES - DA_ROT_DIM // 2, 1),
                            pltpu.roll(y, DA_ROT_DIM // 2, 1))
        return y * cda + partner * sda

    def mla_rope(y):
        partner = jnp.where(m_first, pltpu.roll(y, LANES - MLA_ROPE // 2, 1),
                            pltpu.roll(y, MLA_ROPE // 2, 1))
        return y * cm + partner * sm

    for w_ref, gain_ref, o_ref in ((wq_ref, qg_ref, q_ref), (wk_ref, kg_ref, k_ref)):
        p_all = _dot(h, w_ref[...])
        gain = gain_ref[...]
        for hd in range(DA_HEADS):
            sl = slice(hd * LANES, (hd + 1) * LANES)
            p = p_all[:, sl]
            p2 = p * p
            s_all = jnp.sum(p2, axis=-1, keepdims=True)
            s_lo = jnp.sum(jnp.where(lo, p2, 0.0), axis=-1, keepdims=True)
            inv = jnp.where(lo, lax.rsqrt(s_lo * (1.0 / DA_HEAD_DIM) + NORM_EPS),
                            lax.rsqrt((s_all - s_lo) * (1.0 / DA_HEAD_DIM) + NORM_EPS))
            o_ref[:, sl] = da_rope(p * inv * gain).astype(BF16)
    v_ref[...] = _dot(h, wv_ref[...]).astype(BF16)

    cq = _dot(h, wcq_ref[...])
    cqn = (cq * _rms(cq, MLA_Q_LORA) * qlg_ref[...]).astype(BF16)
    qm = _dot(cqn, wuq_ref[...])
    gn, gr = mqn_ref[...], mqr_ref[...]
    for hd in range(MLA_HEADS):
        sl = slice(hd * LANES, (hd + 1) * LANES)
        qn = qm[:, sl]
        qr = qm[:, D_MODEL + hd * LANES:D_MODEL + (hd + 1) * LANES]
        ss = jnp.sum(qn * qn, axis=-1, keepdims=True) + jnp.sum(qr * qr, axis=-1, keepdims=True)
        inv = lax.rsqrt(ss * (1.0 / (MLA_NOPE + MLA_ROPE)) + NORM_EPS)
        qn_ref[:, sl] = (qn * inv * gn).astype(BF16)
        qr_ref[:, sl] = mla_rope(qr * inv * gr).astype(BF16)

    ckv = _dot(h, wckv_ref[...])
    ckvn = (ckv * _rms(ckv, MLA_KV_LORA) * kvlg_ref[...]).astype(BF16)
    kv = _dot(ckvn, wukv_ref[...])
    gkn = mkn_ref[...]
    for hd in range(MLA_HEADS):
        sl = slice(hd * LANES, (hd + 1) * LANES)
        kn = kv[:, sl]
        kn_ref[:, sl] = (kn * _rms(kn, MLA_NOPE) * gkn).astype(BF16)
    vm_ref[...] = kv[:, D_MODEL:].astype(BF16)
    kr = _dot(h, wkr_ref[...])
    kr_ref[...] = mla_rope(kr * _rms(kr, MLA_ROPE) * mkr_ref[...]).astype(BF16)

    gl = _dot(h, wg_ref[...]) + bg_ref[...]
    gate_ref[...] = (1.0 / (1.0 + jnp.exp(-gl))).astype(BF16)


def _in_proj(x2, g, wts, vecs, tables):
    T = x2.shape[0]
    tm = PROJ_ROWS
    row = lambda w: pl.BlockSpec((tm, w), lambda i: (i, 0))
    in_specs = ([row(D_MODEL), _const_spec(g.shape)] + [_const_spec(w.shape) for w in wts]
                + [_const_spec(v.shape) for v in vecs] + [row(LANES)] * 4)
    outs = [jax.ShapeDtypeStruct((T, D_MODEL), BF16)] * 7 + [
        jax.ShapeDtypeStruct((T, LANES), BF16), jax.ShapeDtypeStruct((T, 2 * D_MODEL), BF16)]
    out_specs = [row(D_MODEL)] * 7 + [row(LANES), row(2 * D_MODEL)]
    return pl.pallas_call(
        _in_proj_kernel, out_shape=outs, grid=(T // tm,), in_specs=in_specs, out_specs=out_specs,
        compiler_params=pltpu.CompilerParams(dimension_semantics=("parallel",),
                                             vmem_limit_bytes=VMEM_LIMIT),
        name="in_proj")(x2, g, *wts, *vecs, *tables)


def _flash(qs, load_k, load_v, qi, tq, n_maps, sa_ref, sb_ref, m_ref, l_ref, acc_ref):
    rows = n_maps * tq
    tk = ATT_TK
    assert tk % tq == 0
    m_ref[...] = jnp.full(m_ref.shape, -jnp.inf, F32)
    l_ref[...] = jnp.zeros(l_ref.shape, F32)
    acc_ref[...] = jnp.zeros(acc_ref.shape, F32)

    def scores(j, s_ref):
        s_ref[...] = _dot_nt(qs, load_k(j))

    def update(j, s_ref, masked):
        s = s_ref[...]
        if masked:
            qpos = qi * tq + (lax.broadcasted_iota(jnp.int32, (rows, tk), 0) & (tq - 1))
            kpos = j * tk + lax.broadcasted_iota(jnp.int32, (rows, tk), 1)
            s = jnp.where(kpos <= qpos, s, -jnp.inf)
        m_old = m_ref[...]
        m_new = jnp.maximum(m_old, jnp.max(s, axis=-1, keepdims=True))
        alpha = jnp.exp2(m_old - m_new)
        p = jnp.exp2(s - m_new)
        psum = p[:, :LANES]
        for c in range(1, tk // LANES):
            psum = psum + p[:, c * LANES:(c + 1) * LANES]
        l_ref[...] = alpha * l_ref[...] + psum
        acc_ref[...] = alpha * acc_ref[...] + _dot(p.astype(BF16), load_v(j))
        m_ref[...] = m_new

    n_full = (qi * tq) // tk
    n_pairs = n_full // 2
    tail2 = n_full - 2 * n_pairs

    scores(0, sa_ref)

    def body(i, carry):
        scores(2 * i + 1, sb_ref)
        update(2 * i, sa_ref, False)
        scores(2 * i + 2, sa_ref)
        update(2 * i + 1, sb_ref, False)
        return carry

    lax.fori_loop(0, n_pairs, body, 0)
    e0 = 2 * n_pairs

    @pl.when(tail2 == 1)
    def _():
        scores(e0 + 1, sb_ref)
        update(e0, sa_ref, False)
        update(e0 + 1, sb_ref, True)

    @pl.when(tail2 == 0)
    def _():
        update(e0, sa_ref, True)

    return acc_ref[...], jnp.sum(l_ref[...], axis=-1, keepdims=True)


def _flash_scratch(rows):
    return [pltpu.VMEM((rows, ATT_TK), F32), pltpu.VMEM((rows, ATT_TK), F32),
            pltpu.VMEM((rows, 1), F32), pltpu.VMEM((rows, LANES), F32),
            pltpu.VMEM((rows, LANES), F32)]


def _kv_block(j):
    return pl.ds(pl.multiple_of(j * ATT_TK, ATT_TK), ATT_TK)


def _da_attn_kernel(q_ref, k_ref, v_ref, lam_ref, sg_ref, o_ref, *scratch, lambda_init):
    qi = pl.program_id(2)
    tq = DA_TQ
    q = q_ref[...]
    lane = lax.broadcasted_iota(jnp.int32, q.shape, 1)
    zero = jnp.zeros_like(q)
    qs = jnp.concatenate([jnp.where(lane < DA_HEAD_DIM, q, zero),
                          jnp.where(lane >= DA_HEAD_DIM, q, zero)], axis=0)

    acc, l = _flash(qs, lambda j: k_ref[_kv_block(j), :], lambda j: v_ref[_kv_block(j), :],
                    qi, tq, 2, *scratch)
    o1 = acc[:tq] / l[:tq]
    o2 = acc[tq:] / l[tq:]
    lv = lam_ref[...]
    lam = (jnp.exp(jnp.sum(lv[0:1] * lv[1:2], axis=-1, keepdims=True))
           - jnp.exp(jnp.sum(lv[2:3] * lv[3:4], axis=-1, keepdims=True)) + lambda_init)
    o = o1 - lam * o2
    y = o * _rms(o, DA_V_DIM) * sg_ref[...] * (1.0 - lambda_init)
    o_ref[...] = y.astype(BF16)


def _da_attn(q, k, v, lam_vecs, subln_g, lambda_init):
    B, S, _ = q.shape
    tq = DA_TQ
    qspec = pl.BlockSpec((None, tq, LANES), lambda b, h, i: (b, i, h))
    kvspec = pl.BlockSpec((None, S, LANES), lambda b, h, i: (b, 0, h))
    return pl.pallas_call(
        functools.partial(_da_attn_kernel, lambda_init=lambda_init),
        out_shape=jax.ShapeDtypeStruct((B, S, D_MODEL), BF16),
        grid=(B, DA_HEADS, S // tq),
        in_specs=[qspec, kvspec, kvspec, _const_spec(lam_vecs.shape), _const_spec(subln_g.shape)],
        out_specs=qspec,
        scratch_shapes=_flash_scratch(2 * tq),
        compiler_params=pltpu.CompilerParams(
            dimension_semantics=("parallel", "parallel", "arbitrary"), vmem_limit_bytes=VMEM_LIMIT),
        name="da_attn")(q, k, v, lam_vecs, subln_g)


def _mla_attn_kernel(qn_ref, qr_ref, kn_ref, kr_ref, v_ref, o_ref, *scratch):
    qi = pl.program_id(2)
    qs = jnp.concatenate([qn_ref[...], qr_ref[...]], axis=-1)

    def load_k(j):
        return jnp.concatenate([kn_ref[_kv_block(j), :], kr_ref[_kv_block(j), :]], axis=-1)

    acc, l = _flash(qs, load_k, lambda j: v_ref[_kv_block(j), :], qi, MLA_TQ, 1, *scratch)
    o_ref[...] = (acc / l).astype(BF16)


def _mla_attn(qn, qr, kn, kr, vm):
    B, S, _ = qn.shape
    tq = MLA_TQ
    qspec = pl.BlockSpec((None, tq, LANES), lambda b, h, i: (b, i, h))
    kvspec = pl.BlockSpec((None, S, LANES), lambda b, h, i: (b, 0, h))
    krspec = pl.BlockSpec((None, S, LANES), lambda b, h, i: (b, 0, 0))
    return pl.pallas_call(
        _mla_attn_kernel,
        out_shape=jax.ShapeDtypeStruct((B, S, D_MODEL), BF16),
        grid=(B, MLA_HEADS, S // tq),
        in_specs=[qspec, qspec, kvspec, krspec, kvspec],
        out_specs=qspec,
        scratch_shapes=_flash_scratch(tq),
        compiler_params=pltpu.CompilerParams(
            dimension_semantics=("parallel", "parallel", "arbitrary"), vmem_limit_bytes=VMEM_LIMIT),
        name="mla_attn")(qn, qr, kn, kr, vm)


def _merge_kernel(oda_ref, omla_ref, gate_ref, x_ref, wo_ref, fg_ref, whi_ref, wlo_ref, br_ref,
                  x1_ref, h2_ref, ri_ref, rw_ref, cnt_ref, carry_ref):
    i = pl.program_id(0)
    rows = x_ref.shape[0]

    @pl.when(i == 0)
    def _():
        carry_ref[...] = jnp.zeros_like(carry_ref)

    g = gate_ref[...].astype(F32)
    mixed = g[:, :D_MODEL] * oda_ref[...].astype(F32) + g[:, D_MODEL:] * omla_ref[...].astype(F32)
    x1 = x_ref[...] + _dot(mixed.astype(BF16), wo_ref[...])
    x1_ref[...] = x1
    h2 = x1 * _rms(x1, D_MODEL) * fg_ref[...]
    for s in range(ROW_CHUNKS):
        h2_ref[pl.ds(s, rows, stride=ROW_CHUNKS), :] = h2[:, s * LANES:(s + 1) * LANES]

    hi = h2.astype(BF16)
    lo = (h2 - hi.astype(F32)).astype(BF16)
    whi = whi_ref[...]
    logits = _dot(hi, whi) + _dot(hi, wlo_ref[...]) + _dot(lo, whi) + br_ref[...]

    lane = lax.broadcasted_iota(jnp.int32, (rows, LANES), 1)
    neg = -jnp.inf
    gl = jnp.where(lane < N_GROUPS, logits, neg)
    gmax = jnp.max(gl, axis=-1, keepdims=True)
    g_sel = jnp.min(jnp.where(gl == gmax, lane, LANES), axis=-1, keepdims=True)
    p_g = 1.0 / jnp.sum(jnp.exp(gl - gmax), axis=-1, keepdims=True)
    e_lo = N_GROUPS + g_sel * EXPERTS_PER_GROUP
    el = jnp.where((lane >= e_lo) & (lane < e_lo + EXPERTS_PER_GROUP), logits, neg)
    v0 = jnp.max(el, axis=-1, keepdims=True)
    i0 = jnp.min(jnp.where(el == v0, lane, LANES), axis=-1, keepdims=True)
    el2 = jnp.where(lane == i0, neg, el)
    v1 = jnp.max(el2, axis=-1, keepdims=True)
    i1 = jnp.min(jnp.where(el2 == v1, lane, LANES), axis=-1, keepdims=True)
    t = jnp.exp(v1 - v0)
    w0 = p_g / (1.0 + t)
    w1 = p_g * t / (1.0 + t)
    e0 = i0 - N_GROUPS
    e1 = i1 - N_GROUPS

    oh = jnp.where((lane == e0) | (lane == e1), 1.0, 0.0)
    r_i = lax.broadcasted_iota(jnp.int32, (rows, rows), 0)
    c_i = lax.broadcasted_iota(jnp.int32, (rows, rows), 1)
    tri = jnp.where(c_i < r_i, 1.0, 0.0).astype(BF16)
    before = _dot(tri, oh.astype(BF16)) + carry_ref[...]
    rank0 = jnp.sum(jnp.where(lane == e0, before, 0.0), axis=-1, keepdims=True).astype(jnp.int32)
    rank1 = jnp.sum(jnp.where(lane == e1, before, 0.0), axis=-1, keepdims=True).astype(jnp.int32)
    carry = carry_ref[...] + jnp.sum(oh, axis=0, keepdims=True)
    carry_ref[...] = carry
    cnt_ref[...] = carry

    ri = jnp.where(lane == 0, e0, jnp.where(lane == 1, e1, jnp.where(lane == 2, rank0, rank1)))
    ri_ref[...] = ri[:, :SUBLANES]
    rw_ref[...] = jnp.where(lane == 0, w0, w1)[:, :SUBLANES]


def _merge(oda, omla, gates, x2, wo, fg, whi, wlo, br):
    T = x2.shape[0]
    tm = PROJ_ROWS
    row = lambda w: pl.BlockSpec((tm, w), lambda i: (i, 0))
    outs = [jax.ShapeDtypeStruct((T, D_MODEL), F32),
            jax.ShapeDtypeStruct((T * ROW_CHUNKS, LANES), F32),
            jax.ShapeDtypeStruct((T, SUBLANES), jnp.int32),
            jax.ShapeDtypeStruct((T, SUBLANES), F32),
            jax.ShapeDtypeStruct((1, LANES), F32)]
    out_specs = [row(D_MODEL), pl.BlockSpec((tm * ROW_CHUNKS, LANES), lambda i: (i, 0)),
                 row(SUBLANES), row(SUBLANES), pl.BlockSpec((1, LANES), lambda i: (0, 0))]
    return pl.pallas_call(
        _merge_kernel, out_shape=outs, grid=(T // tm,),
        in_specs=[row(D_MODEL), row(D_MODEL), row(2 * D_MODEL), row(D_MODEL), _const_spec(wo.shape),
                  _const_spec(fg.shape), _const_spec(whi.shape), _const_spec(wlo.shape),
                  _const_spec(br.shape)],
        out_specs=out_specs,
        scratch_shapes=[pltpu.VMEM((1, LANES), F32)],
        compiler_params=pltpu.CompilerParams(dimension_semantics=("arbitrary",),
                                             vmem_limit_bytes=VMEM_LIMIT),
        name="merge_router")(oda, omla, gates, x2, wo, fg, whi, wlo, br)


def _row(ref, r):
    return ref.at[pl.ds(pl.multiple_of(r * ROW_CHUNKS, ROW_CHUNKS), ROW_CHUNKS), :]


def _expert_kernel(ce_ref, nu_ref, tok_ref, tok_next_ref, h2_ref, w1_ref, w3_ref, w2_ref, yb_ref,
                   buf_ref, sem):
    c = pl.program_id(0)
    ch = MOE_CHUNK
    slot = c & 1
    n_used = nu_ref[0]

    def gather(toks, sl):
        def issue(t, carry):
            pltpu.make_async_copy(_row(h2_ref, toks[0, 0, t]), _row(buf_ref.at[sl], t),
                                  sem.at[sl]).start()
            return carry

        lax.fori_loop(0, ch, issue, 0, unroll=8)

    @pl.when(c == 0)
    def _():
        gather(tok_ref, 0)

    @pl.when(c + 1 < n_used)
    def _():
        gather(tok_next_ref, 1 - slot)

    @pl.when(c < n_used)
    def _():
        pltpu.make_async_copy(h2_ref.at[pl.ds(0, ch * ROW_CHUNKS), :], buf_ref.at[slot],
                              sem.at[slot]).wait()
        x = jnp.concatenate([buf_ref[slot, pl.ds(s, ch, stride=ROW_CHUNKS), :]
                             for s in range(ROW_CHUNKS)], axis=-1).astype(BF16)
        a = _dot(x, w1_ref[...])
        b = _dot(x, w3_ref[...])
        hmid = (a / (1.0 + jnp.exp(-a))) * b
        y = _dot(hmid.astype(BF16), w2_ref[...])
        for s in range(ROW_CHUNKS):
            yb_ref[pl.ds(s, ch, stride=ROW_CHUNKS), :] = y[:, s * LANES:(s + 1) * LANES]

    @pl.when(c >= n_used)
    def _():
        yb_ref[...] = jnp.zeros_like(yb_ref)


def _experts(chunk_e, n_used, tok3, h2r, w1, w3, w2):
    ch = MOE_CHUNK
    n_chunks = tok3.shape[0]
    smem = lambda f: pl.BlockSpec((1, 1, ch), f, memory_space=pltpu.SMEM)
    wspec = lambda s: pl.BlockSpec((None,) + s, lambda c, ce, nu: (ce[c], 0, 0))
    return pl.pallas_call(
        _expert_kernel,
        out_shape=jax.ShapeDtypeStruct((n_chunks * ch * ROW_CHUNKS, LANES), F32),
        grid_spec=pltpu.PrefetchScalarGridSpec(
            num_scalar_prefetch=2, grid=(n_chunks,),
            in_specs=[smem(lambda c, ce, nu: (c, 0, 0)),
                      smem(lambda c, ce, nu: (jnp.minimum(c + 1, n_chunks - 1), 0, 0)),
                      pl.BlockSpec(memory_space=pl.ANY),
                      wspec((D_MODEL, D_EXPERT)), wspec((D_MODEL, D_EXPERT)),
                      wspec((D_EXPERT, D_MODEL))],
            out_specs=pl.BlockSpec((ch * ROW_CHUNKS, LANES), lambda c, ce, nu: (c, 0)),
            scratch_shapes=[pltpu.VMEM((2, ch * ROW_CHUNKS, LANES), F32),
                            pltpu.SemaphoreType.DMA((2,))]),
        compiler_params=pltpu.CompilerParams(dimension_semantics=("arbitrary",),
                                             vmem_limit_bytes=VMEM_LIMIT),
        name="experts")(chunk_e, n_used, tok3, tok3, h2r, w1, w3, w2)


def _combine_kernel(dest_ref, yb_ref, x1_ref, rw_ref, o_ref, buf_ref, sem):
    tb = ROUTE_ROWS

    def issue(t, c):
        for k in range(TOP_K):
            pltpu.make_async_copy(_row(yb_ref, dest_ref[0, 0, 2 * t + k]),
                                  _row(buf_ref.at[k], t), sem).start()
        return c

    lax.fori_loop(0, tb, issue, 0)

    def drain(t, c):
        pltpu.make_async_copy(_row(yb_ref, 0), _row(buf_ref.at[0], 0), sem).wait()
        return c

    lax.fori_loop(0, TOP_K * tb, drain, 0)
    rw = rw_ref[...]
    acc = x1_ref[...]
    for k in range(TOP_K):
        y = jnp.concatenate([buf_ref[k, pl.ds(s, tb, stride=ROW_CHUNKS), :] for s in range(ROW_CHUNKS)],
                            axis=-1)
        acc = acc + y * rw[:, k:k + 1]
    o_ref[...] = acc


def _combine(dest3, yb, x1, rw):
    T = x1.shape[0]
    tb = ROUTE_ROWS
    return pl.pallas_call(
        _combine_kernel,
        out_shape=jax.ShapeDtypeStruct((T, D_MODEL), F32),
        grid=(T // tb,),
        in_specs=[pl.BlockSpec((1, 1, 2 * tb), lambda i: (i, 0, 0), memory_space=pltpu.SMEM),
                  pl.BlockSpec(memory_space=pl.ANY),
                  pl.BlockSpec((tb, D_MODEL), lambda i: (i, 0)),
                  pl.BlockSpec((tb, SUBLANES), lambda i: (i, 0))],
        out_specs=pl.BlockSpec((tb, D_MODEL), lambda i: (i, 0)),
        scratch_shapes=[pltpu.VMEM((TOP_K, tb * ROW_CHUNKS, LANES), F32), pltpu.SemaphoreType.DMA],
        compiler_params=pltpu.CompilerParams(dimension_semantics=("arbitrary",),
                                             vmem_limit_bytes=VMEM_LIMIT),
        name="combine")(dest3, yb, x1, rw)


def _rope_tables(positions, dim, half_lanes, one_fill):
    inv = ROPE_THETA ** (-jnp.arange(0, dim, 2, dtype=F32) / dim)
    ang = positions.astype(F32).reshape(-1, 1) * inv
    c, s = jnp.cos(ang), jnp.sin(ang)
    T = ang.shape[0]
    fill = half_lanes - dim
    cseg = jnp.concatenate([c, c, jnp.full((T, fill), one_fill, F32)], axis=-1)
    sseg = jnp.concatenate([-s, s, jnp.zeros((T, fill), F32)], axis=-1)
    reps = LANES // half_lanes
    return jnp.tile(cseg, (1, reps)), jnp.tile(sseg, (1, reps))


def _layer(x2, B, S, tables, lambda_init, attn_norm_g, w_in, b_gate, da_q_norm_g, da_k_norm_g,
           da_lambda_q1, da_lambda_k1, da_lambda_q2, da_lambda_k2, da_subln_g, mla_q_lora_g,
           mla_w_uq, mla_kv_lora_g, mla_w_ukv, mla_q_norm_g, mla_k_nope_norm_g, mla_k_rope_norm_g,
           w_o, ffn_norm_g, w_group, b_group, w_router, b_router, w1, w3, w2):
    T = B * S
    D = D_MODEL
    c0, c1, c2 = D, 2 * D, 3 * D
    c3 = c2 + MLA_Q_LORA
    c4 = c3 + MLA_KV_LORA
    c5 = c4 + MLA_ROPE
    wb = w_in.astype(BF16)
    wkr = jnp.pad(wb[:, c4:c5], ((0, 0), (0, LANES - MLA_ROPE)))
    qk = MLA_NOPE + MLA_ROPE
    uq = mla_w_uq.astype(BF16).reshape(MLA_Q_LORA, MLA_HEADS, qk)
    wuq = jnp.concatenate([
        uq[:, :, :MLA_NOPE].reshape(MLA_Q_LORA, D),
        jnp.pad(uq[:, :, MLA_NOPE:], ((0, 0), (0, 0), (0, LANES - MLA_ROPE))).reshape(MLA_Q_LORA, D)],
        axis=1)
    ukv = mla_w_ukv.astype(BF16).reshape(MLA_KV_LORA, MLA_HEADS, MLA_NOPE + MLA_V)
    wukv = jnp.concatenate([ukv[:, :, :MLA_NOPE].reshape(MLA_KV_LORA, D),
                            ukv[:, :, MLA_NOPE:].reshape(MLA_KV_LORA, D)], axis=1)
    wts = (wb[:, :c0], wb[:, c0:c1], wb[:, c1:c2], wb[:, c2:c3], wb[:, c3:c4], wkr, wb[:, c5:],
           wuq, wukv)
    r1 = lambda v: v.astype(F32).reshape(1, -1)
    pad_rope = lambda v: jnp.pad(v.astype(F32), (0, LANES - MLA_ROPE)).reshape(1, LANES)
    da_scale = DA_HEAD_DIM ** -0.5 * LOG2_E
    mla_scale = qk ** -0.5 * LOG2_E
    vecs = (r1(b_gate),
            jnp.tile(r1(da_q_norm_g), (1, 2)) * da_scale,
            jnp.tile(r1(da_k_norm_g), (1, 2)),
            r1(mla_q_lora_g), r1(mla_kv_lora_g),
            r1(mla_q_norm_g[:MLA_NOPE]) * mla_scale,
            pad_rope(mla_q_norm_g[MLA_NOPE:]) * mla_scale,
            r1(mla_k_nope_norm_g), pad_rope(mla_k_rope_norm_g))
    q, k, v, qn, qr, kn, vm, kr, gates = _in_proj(x2, r1(attn_norm_g), wts, vecs, tables)

    b3 = lambda a: a.reshape(B, S, a.shape[-1])
    lam_vecs = jnp.stack([da_lambda_q1, da_lambda_k1, da_lambda_q2, da_lambda_k2]).astype(F32)
    o_da = _da_attn(b3(q), b3(k), b3(v), lam_vecs, r1(da_subln_g), lambda_init)
    o_mla = _mla_attn(b3(qn), b3(qr), b3(kn), b3(kr), b3(vm))

    wgr = jnp.concatenate([w_group.astype(F32), w_router.astype(F32),
                           jnp.zeros((D, LANES - N_GROUPS - N_EXPERTS), F32)], axis=1)
    whi = wgr.astype(BF16)
    wlo = (wgr - whi.astype(F32)).astype(BF16)
    br = jnp.concatenate([b_group.astype(F32), b_router.astype(F32),
                          jnp.zeros((LANES - N_GROUPS - N_EXPERTS,), F32)]).reshape(1, LANES)
    x1, h2r, ri, rw, cnt = _merge(o_da.reshape(T, D), o_mla.reshape(T, D), gates, x2,
                                  w_o.astype(BF16), r1(ffn_norm_g), whi, wlo, br)

    ch = MOE_CHUNK
    counts = cnt[0, :N_EXPERTS].astype(jnp.int32)
    padded = (counts + ch - 1) // ch * ch
    pends = jnp.cumsum(padded)
    pstarts = pends - padded
    n_chunks = (T * TOP_K) // ch + N_EXPERTS
    P = n_chunks * ch
    dest = pstarts[ri[:, :TOP_K]] + ri[:, TOP_K:2 * TOP_K]
    dest3 = dest.reshape(T // ROUTE_ROWS, 1, TOP_K * ROUTE_ROWS)
    chunk_start = jnp.arange(n_chunks, dtype=jnp.int32)[:, None] * ch
    chunk_e = jnp.minimum(jnp.sum((pends[None, :] <= chunk_start).astype(jnp.int32), axis=1),
                          N_EXPERTS - 1).astype(jnp.int32)
    n_used = (pends[-1:] // ch).astype(jnp.int32)
    tok = jnp.repeat(jnp.arange(T, dtype=jnp.int32), TOP_K)
    tok3 = jnp.zeros((P,), jnp.int32).at[dest.reshape(-1)].set(tok).reshape(n_chunks, 1, ch)

    yb = _experts(chunk_e, n_used, tok3, h2r, w1.astype(BF16), w3.astype(BF16), w2.astype(BF16))
    return _combine(dest3, yb, x1, rw)


def kernel(x, positions, attn_norm_g, w_in, b_gate, da_q_norm_g, da_k_norm_g, da_lambda_q1, da_lambda_k1, da_lambda_q2, da_lambda_k2, da_subln_g, mla_q_lora_g, mla_w_uq, mla_kv_lora_g, mla_w_ukv, mla_q_norm_g, mla_k_nope_norm_g, mla_k_rope_norm_g, w_o, ffn_norm_g, w_group, b_group, w_router, b_router, w1, w3, w2):
    B, S, D = x.shape
    assert D == D_MODEL and S % MLA_TQ == 0 and (B * S) % PROJ_ROWS == 0
    tables = (_rope_tables(positions, DA_ROT_DIM, DA_HEAD_DIM, 1.0)
              + _rope_tables(positions, MLA_ROPE, LANES, 1.0))
    x2 = x.reshape(B * S, D)
    per_layer = (attn_norm_g, w_in, b_gate, da_q_norm_g, da_k_norm_g, da_lambda_q1, da_lambda_k1,
                 da_lambda_q2, da_lambda_k2, da_subln_g, mla_q_lora_g, mla_w_uq, mla_kv_lora_g,
                 mla_w_ukv, mla_q_norm_g, mla_k_nope_norm_g, mla_k_rope_norm_g, w_o, ffn_norm_g,
                 w_group, b_group, w_router, b_router, w1, w3, w2)
    for l in range(w_in.shape[0]):
        lambda_init = 0.8 - 0.6 * math.exp(-0.3 * l)
        x2 = _layer(x2, B, S, tables, lambda_init, *[p[l] for p in per_layer])
    return x2.reshape(B, S, D)
```

```python
import functools
import math

import jax
import jax.numpy as jnp
from jax import lax
from jax.experimental import pallas as pl
from jax.experimental.pallas import tpu as pltpu

D_MODEL = 1024
DA_HEADS = 8
DA_HEAD_DIM = 64
DA_V_DIM = 128
DA_ROT_DIM = 16
MLA_HEADS = 8
MLA_Q_LORA = 384
MLA_KV_LORA = 256
MLA_NOPE = 128
MLA_ROPE = 64
MLA_V = 128
ROPE_THETA = 500000.0
N_GROUPS = 4
EXPERTS_PER_GROUP = 8
N_EXPERTS = N_GROUPS * EXPERTS_PER_GROUP
TOP_K = 2
D_EXPERT = 512
NORM_EPS = 1e-6
LOG2_E = math.log2(math.e)

LANES = 128
SUBLANES = 8
ROW_CHUNKS = D_MODEL // LANES

PROJ_ROWS = 256
DA_TQ = 512
MLA_TQ = 512
ATT_TK = 512
MOE_CHUNK = 256
ROUTE_ROWS = 256
VMEM_LIMIT = 48 * 1024 * 1024

F32 = jnp.float32
BF16 = jnp.bfloat16


def _const_spec(shape):
    nd = len(shape)
    return pl.BlockSpec(shape, lambda *_: (0,) * nd, pipeline_mode=pl.Buffered(1))


def _rms(v, n):
    return lax.rsqrt(jnp.sum(v * v, axis=-1, keepdims=True) * (1.0 / n) + NORM_EPS)


def _dot(a, b):
    return jnp.dot(a, b, preferred_element_type=F32)


def _dot_nt(a, b):
    return lax.dot_general(a, b, (((1,), (1,)), ((), ())), preferred_element_type=F32)


def _in_proj_kernel(x_ref, g_ref, wq_ref, wk_ref, wv_ref, wcq_ref, wckv_ref, wkr_ref, wg_ref,
                    wuq_ref, wukv_ref, bg_ref, qg_ref, kg_ref, qlg_ref, kvlg_ref, mqn_ref, mqr_ref,
                    mkn_ref, mkr_ref, cda_ref, sda_ref, cm_ref, sm_ref,
                    q_ref, k_ref, v_ref, qn_ref, qr_ref, kn_ref, vm_ref, kr_ref, gate_ref):
    x = x_ref[...]
    h = (x * _rms(x, D_MODEL) * g_ref[...]).astype(BF16)
    rows = x.shape[0]
    lane = lax.broadcasted_iota(jnp.int32, (rows, LANES), 1)
    lo = lane < DA_HEAD_DIM
    da_first = (lane & (DA_HEAD_DIM - 1)) < (DA_ROT_DIM // 2)
    m_first = lane < (MLA_ROPE // 2)
    cda, sda = cda_ref[...], sda_ref[...]
    cm, sm = cm_ref[...], sm_ref[...]

    def da_rope(y):
        partner = jnp.where(da_first, pltpu.roll(y, LANES - DA_ROT_DIM // 2, 1),
                            pltpu.roll(y, DA_ROT_DIM // 2, 1))
        return y * cda + partner * sda

    def mla_rope(y):
        partner = jnp.where(m_first, pltpu.roll(y, LANES - MLA_ROPE // 2, 1),
                            pltpu.roll(y, MLA_ROPE // 2, 1))
        return y * cm + partner * sm

    for w_ref, gain_ref, o_ref in ((wq_ref, qg_ref, q_ref), (wk_ref, kg_ref, k_ref)):
        p_all = _dot(h, w_ref[...])
        gain = gain_ref[...]
        for hd in range(DA_HEADS):
            sl = slice(hd * LANES, (hd + 1) * LANES)
            p = p_all[:, sl]
            p2 = p * p
            s_all = jnp.sum(p2, axis=-1, keepdims=True)
            s_lo = jnp.sum(jnp.where(lo, p2, 0.0), axis=-1, keepdims=True)
            inv = jnp.where(lo, lax.rsqrt(s_lo * (1.0 / DA_HEAD_DIM) + NORM_EPS),
                            lax.rsqrt((s_all - s_lo) * (1.0 / DA_HEAD_DIM) + NORM_EPS))
            o_ref[:, sl] = da_rope(p * inv * gain).astype(BF16)
    v_ref[...] = _dot(h, wv_ref[...]).astype(BF16)

    cq = _dot(h, wcq_ref[...])
    cqn = (cq * _rms(cq, MLA_Q_LORA) * qlg_ref[...]).astype(BF16)
    qm = _dot(cqn, wuq_ref[...])
    gn, gr = mqn_ref[...], mqr_ref[...]
    for hd in range(MLA_HEADS):
        sl = slice(hd * LANES, (hd + 1) * LANES)
        qn = qm[:, sl]
        qr = qm[:, D_MODEL + hd * LANES:D_MODEL + (hd + 1) * LANES]
        ss = jnp.sum(qn * qn, axis=-1, keepdims=True) + jnp.sum(qr * qr, axis=-1, keepdims=True)
        inv = lax.rsqrt(ss * (1.0 / (MLA_NOPE + MLA_ROPE)) + NORM_EPS)
        qn_ref[:, sl] = (qn * inv * gn).astype(BF16)
        qr_ref[:, sl] = mla_rope(qr * inv * gr).astype(BF16)

    ckv = _dot(h, wckv_ref[...])
    ckvn = (ckv * _rms(ckv, MLA_KV_LORA) * kvlg_ref[...]).astype(BF16)
    kv = _dot(ckvn, wukv_ref[...])
    gkn = mkn_ref[...]
    for hd in range(MLA_HEADS):
        sl = slice(hd * LANES, (hd + 1) * LANES)
        kn = kv[:, sl]
        kn_ref[:, sl] = (kn * _rms(kn, MLA_NOPE) * gkn).astype(BF16)
    vm_ref[...] = kv[:, D_MODEL:].astype(BF16)
    kr = _dot(h, wkr_ref[...])
    kr_ref[...] = mla_rope(kr * _rms(kr, MLA_ROPE) * mkr_ref[...]).astype(BF16)

    gl = _dot(h, wg_ref[...]) + bg_ref[...]
    gate_ref[...] = (1.0 / (1.0 + jnp.exp(-gl))).astype(BF16)


def _in_proj(x2, g, wts, vecs, tables):
    T = x2.shape[0]
    tm = PROJ_ROWS
    row = lambda w: pl.BlockSpec((tm, w), lambda i: (i, 0))
    in_specs = ([row(D_MODEL), _const_spec(g.shape)] + [_const_spec(w.shape) for w in wts]
                + [_const_spec(v.shape) for v in vecs] + [row(LANES)] * 4)
    outs = [jax.ShapeDtypeStruct((T, D_MODEL), BF16)] * 7 + [
        jax.ShapeDtypeStruct((T, LANES), BF16), jax.ShapeDtypeStruct((T, 2 * D_MODEL), BF16)]
    out_specs = [row(D_MODEL)] * 7 + [row(LANES), row(2 * D_MODEL)]
    return pl.pallas_call(
        _in_proj_kernel, out_shape=outs, grid=(T // tm,), in_specs=in_specs, out_specs=out_specs,
        compiler_params=pltpu.CompilerParams(dimension_semantics=("parallel",),
                                             vmem_limit_bytes=VMEM_LIMIT),
        name="in_proj")(x2, g, *wts, *vecs, *tables)


def _flash(qs, load_k, load_v, qi, tq, n_maps, sa_ref, sb_ref, m_ref, l_ref, acc_ref):
    rows = n_maps * tq
    tk = ATT_TK
    assert tk % tq == 0
    m_ref[...] = jnp.full(m_ref.shape, -jnp.inf, F32)
    l_ref[...] = jnp.zeros(l_ref.shape, F32)
    acc_ref[...] = jnp.zeros(acc_ref.shape, F32)

    def scores(j, s_ref):
        s_ref[...] = _dot_nt(qs, load_k(j))

    def update(j, s_ref, masked):
        s = s_ref[...]
        if masked:
            qpos = qi * tq + (lax.broadcasted_iota(jnp.int32, (rows, tk), 0) & (tq - 1))
            kpos = j * tk + lax.broadcasted_iota(jnp.int32, (rows, tk), 1)
            s = jnp.where(kpos <= qpos, s, -jnp.inf)
        m_old = m_ref[...]
        m_new = jnp.maximum(m_old, jnp.max(s, axis=-1, keepdims=True))
        alpha = jnp.exp2(m_old - m_new)
        p = jnp.exp2(s - m_new)
        psum = p[:, :LANES]
        for c in range(1, tk // LANES):
            psum = psum + p[:, c * LANES:(c + 1) * LANES]
        l_ref[...] = alpha * l_ref[...] + psum
        acc_ref[...] = alpha * acc_ref[...] + _dot(p.astype(BF16), load_v(j))
        m_ref[...] = m_new

    n_full = (qi * tq) // tk
    n_pairs = n_full // 2
    tail2 = n_full - 2 * n_pairs

    scores(0, sa_ref)

    def body(i, carry):
        scores(2 * i + 1, sb_ref)
        update(2 * i, sa_ref, False)
        scores(2 * i + 2, sa_ref)
        update(2 * i + 1, sb_ref, False)
        return carry

    lax.fori_loop(0, n_pairs, body, 0)
    e0 = 2 * n_pairs

    @pl.when(tail2 == 1)
    def _():
        scores(e0 + 1, sb_ref)
        update(e0, sa_ref, False)
        update(e0 + 1, sb_ref, True)

    @pl.when(tail2 == 0)
    def _():
        update(e0, sa_ref, True)

    return acc_ref[...], jnp.sum(l_ref[...], axis=-1, keepdims=True)


def _flash_scratch(rows):
    return [pltpu.VMEM((rows, ATT_TK), F32), pltpu.VMEM((rows, ATT_TK), F32),
            pltpu.VMEM((rows, 1), F32), pltpu.VMEM((rows, LANES), F32),
            pltpu.VMEM((rows, LANES), F32)]


def _kv_block(j):
    return pl.ds(pl.multiple_of(j * ATT_TK, ATT_TK), ATT_TK)


def _da_attn_kernel(q_ref, k_ref, v_ref, lam_ref, sg_ref, o_ref, *scratch, lambda_init):
    qi = pl.program_id(2)
    tq = DA_TQ
    q = q_ref[...]
    lane = lax.broadcasted_iota(jnp.int32, q.shape, 1)
    zero = jnp.zeros_like(q)
    qs = jnp.concatenate([jnp.where(lane < DA_HEAD_DIM, q, zero),
                          jnp.where(lane >= DA_HEAD_DIM, q, zero)], axis=0)

    acc, l = _flash(qs, lambda j: k_ref[_kv_block(j), :], lambda j: v_ref[_kv_block(j), :],
                    qi, tq, 2, *scratch)
    o1 = acc[:tq] / l[:tq]
    o2 = acc[tq:] / l[tq:]
    lv = lam_ref[...]
    lam = (jnp.exp(jnp.sum(lv[0:1] * lv[1:2], axis=-1, keepdims=True))
           - jnp.exp(jnp.sum(lv[2:3] * lv[3:4], axis=-1, keepdims=True)) + lambda_init)
    o = o1 - lam * o2
    y = o * _rms(o, DA_V_DIM) * sg_ref[...] * (1.0 - lambda_init)
    o_ref[...] = y.astype(BF16)


def _da_attn(q, k, v, lam_vecs, subln_g, lambda_init):
    B, S, _ = q.shape
    tq = DA_TQ
    qspec = pl.BlockSpec((None, tq, LANES), lambda b, h, i: (b, i, h))
    kvspec = pl.BlockSpec((None, S, LANES), lambda b, h, i: (b, 0, h))
    return pl.pallas_call(
        functools.partial(_da_attn_kernel, lambda_init=lambda_init),
        out_shape=jax.ShapeDtypeStruct((B, S, D_MODEL), BF16),
        grid=(B, DA_HEADS, S // tq),
        in_specs=[qspec, kvspec, kvspec, _const_spec(lam_vecs.shape), _const_spec(subln_g.shape)],
        out_specs=qspec,
        scratch_shapes=_flash_scratch(2 * tq),
        compiler_params=pltpu.CompilerParams(
            dimension_semantics=("parallel", "parallel", "arbitrary"), vmem_limit_bytes=VMEM_LIMIT),
        name="da_attn")(q, k, v, lam_vecs, subln_g)


def _mla_attn_kernel(qn_ref, qr_ref, kn_ref, kr_ref, v_ref, o_ref, *scratch):
    qi = pl.program_id(2)
    qs = jnp.concatenate([qn_ref[...], qr_ref[...]], axis=-1)

    def load_k(j):
        return jnp.concatenate([kn_ref[_kv_block(j), :], kr_ref[_kv_block(j), :]], axis=-1)

    acc, l = _flash(qs, load_k, lambda j: v_ref[_kv_block(j), :], qi, MLA_TQ, 1, *scratch)
    o_ref[...] = (acc / l).astype(BF16)


def _mla_attn(qn, qr, kn, kr, vm):
    B, S, _ = qn.shape
    tq = MLA_TQ
    qspec = pl.BlockSpec((None, tq, LANES), lambda b, h, i: (b, i, h))
    kvspec = pl.BlockSpec((None, S, LANES), lambda b, h, i: (b, 0, h))
    krspec = pl.BlockSpec((None, S, LANES), lambda b, h, i: (b, 0, 0))
    return pl.pallas_call(
        _mla_attn_kernel,
        out_shape=jax.ShapeDtypeStruct((B, S, D_MODEL), BF16),
        grid=(B, MLA_HEADS, S // tq),
        in_specs=[qspec, qspec, kvspec, krspec, kvspec],
        out_specs=qspec,
        scratch_shapes=_flash_scratch(tq),
        compiler_params=pltpu.CompilerParams(
            dimension_semantics=("parallel", "parallel", "arbitrary"), vmem_limit_bytes=VMEM_LIMIT),
        name="mla_attn")(qn, qr, kn, kr, vm)


def _merge_kernel(oda_ref, omla_ref, gate_ref, x_ref, wo_ref, fg_ref, whi_ref, wlo_ref, br_ref,
                  x1_ref, h2_ref, ri_ref, rw_ref, cnt_ref, carry_ref):
    i = pl.program_id(0)
    rows = x_ref.shape[0]

    @pl.when(i == 0)
    def _():
        carry_ref[...] = jnp.zeros_like(carry_ref)

    g = gate_ref[...].astype(F32)
    mixed = g[:, :D_MODEL] * oda_ref[...].astype(F32) + g[:, D_MODEL:] * omla_ref[...].astype(F32)
    x1 = x_ref[...] + _dot(mixed.astype(BF16), wo_ref[...])
    x1_ref[...] = x1
    h2 = x1 * _rms(x1, D_MODEL) * fg_ref[...]
    for s in range(ROW_CHUNKS):
        h2_ref[pl.ds(s, rows, stride=ROW_CHUNKS), :] = h2[:, s * LANES:(s + 1) * LANES]

    hi = h2.astype(BF16)
    lo = (h2 - hi.astype(F32)).astype(BF16)
    whi = whi_ref[...]
    logits = _dot(hi, whi) + _dot(hi, wlo_ref[...]) + _dot(lo, whi) + br_ref[...]

    lane = lax.broadcasted_iota(jnp.int32, (rows, LANES), 1)
    neg = -jnp.inf
    gl = jnp.where(lane < N_GROUPS, logits, neg)
    gmax = jnp.max(gl, axis=-1, keepdims=True)
    g_sel = jnp.min(jnp.where(gl == gmax, lane, LANES), axis=-1, keepdims=True)
    p_g = 1.0 / jnp.sum(jnp.exp(gl - gmax), axis=-1, keepdims=True)
    e_lo = N_GROUPS + g_sel * EXPERTS_PER_GROUP
    el = jnp.where((lane >= e_lo) & (lane < e_lo + EXPERTS_PER_GROUP), logits, neg)
    v0 = jnp.max(el, axis=-1, keepdims=True)
    i0 = jnp.min(jnp.where(el == v0, lane, LANES), axis=-1, keepdims=True)
    el2 = jnp.where(lane == i0, neg, el)
    v1 = jnp.max(el2, axis=-1, keepdims=True)
    i1 = jnp.min(jnp.where(el2 == v1, lane, LANES), axis=-1, keepdims=True)
    t = jnp.exp(v1 - v0)
    w0 = p_g / (1.0 + t)
    w1 = p_g * t / (1.0 + t)
    e0 = i0 - N_GROUPS
    e1 = i1 - N_GROUPS

    oh = jnp.where((lane == e0) | (lane == e1), 1.0, 0.0)
    r_i = lax.broadcasted_iota(jnp.int32, (rows, rows), 0)
    c_i = lax.broadcasted_iota(jnp.int32, (rows, rows), 1)
    tri = jnp.where(c_i < r_i, 1.0, 0.0).astype(BF16)
    before = _dot(tri, oh.astype(BF16)) + carry_ref[...]
    rank0 = jnp.sum(jnp.where(lane == e0, before, 0.0), axis=-1, keepdims=True).astype(jnp.int32)
    rank1 = jnp.sum(jnp.where(lane == e1, before, 0.0), axis=-1, keepdims=True).astype(jnp.int32)
    carry = carry_ref[...] + jnp.sum(oh, axis=0, keepdims=True)
    carry_ref[...] = carry
    cnt_ref[...] = carry

    ri = jnp.where(lane == 0, e0, jnp.where(lane == 1, e1, jnp.where(lane == 2, rank0, rank1)))
    ri_ref[...] = ri[:, :SUBLANES]
    rw_ref[...] = jnp.where(lane == 0, w0, w1)[:, :SUBLANES]


def _merge(oda, omla, gates, x2, wo, fg, whi, wlo, br):
    T = x2.shape[0]
    tm = PROJ_ROWS
    row = lambda w: pl.BlockSpec((tm, w), lambda i: (i, 0))
    outs = [jax.ShapeDtypeStruct((T, D_MODEL), F32),
            jax.ShapeDtypeStruct((T * ROW_CHUNKS, LANES), F32),
            jax.ShapeDtypeStruct((T, SUBLANES), jnp.int32),
            jax.ShapeDtypeStruct((T, SUBLANES), F32),
            jax.ShapeDtypeStruct((1, LANES), F32)]
    out_specs = [row(D_MODEL), pl.BlockSpec((tm * ROW_CHUNKS, LANES), lambda i: (i, 0)),
                 row(SUBLANES), row(SUBLANES), pl.BlockSpec((1, LANES), lambda i: (0, 0))]
    return pl.pallas_call(
        _merge_kernel, out_shape=outs, grid=(T // tm,),
        in_specs=[row(D_MODEL), row(D_MODEL), row(2 * D_MODEL), row(D_MODEL), _const_spec(wo.shape),
                  _const_spec(fg.shape), _const_spec(whi.shape), _const_spec(wlo.shape),
                  _const_spec(br.shape)],
        out_specs=out_specs,
        scratch_shapes=[pltpu.VMEM((1, LANES), F32)],
        compiler_params=pltpu.CompilerParams(dimension_semantics=("arbitrary",),
                                             vmem_limit_bytes=VMEM_LIMIT),
        name="merge_router")(oda, omla, gates, x2, wo, fg, whi, wlo, br)


def _row(ref, r):
    return ref.at[pl.ds(pl.multiple_of(r * ROW_CHUNKS, ROW_CHUNKS), ROW_CHUNKS), :]


def _expert_kernel(ce_ref, nu_ref, tok_ref, tok_next_ref, h2_ref, w1_ref, w3_ref, w2_ref, yb_ref,
                   buf_ref, sem):
    c = pl.program_id(0)
    ch = MOE_CHUNK
    slot = c & 1
    n_used = nu_ref[0]

    def gather(toks, sl):
        def issue(t, carry):
            pltpu.make_async_copy(_row(h2_ref, toks[0, 0, t]), _row(buf_ref.at[sl], t),
                                  sem.at[sl]).start()
            return carry

        lax.fori_loop(0, ch, issue, 0, unroll=8)

    @pl.when(c == 0)
    def _():
        gather(tok_ref, 0)

    @pl.when(c + 1 < n_used)
    def _():
        gather(tok_next_ref, 1 - slot)

    @pl.when(c < n_used)
    def _():
        pltpu.make_async_copy(h2_ref.at[pl.ds(0, ch * ROW_CHUNKS), :], buf_ref.at[slot],
                              sem.at[slot]).wait()
        x = jnp.concatenate([buf_ref[slot, pl.ds(s, ch, stride=ROW_CHUNKS), :]
                             for s in range(ROW_CHUNKS)], axis=-1).astype(BF16)
        a = _dot(x, w1_ref[...])
        b = _dot(x, w3_ref[...])
        hmid = (a / (1.0 + jnp.exp(-a))) * b
        y = _dot(hmid.astype(BF16), w2_ref[...])
        for s in range(ROW_CHUNKS):
            yb_ref[pl.ds(s, ch, stride=ROW_CHUNKS), :] = y[:, s * LANES:(s + 1) * LANES]

    @pl.when(c >= n_used)
    def _():
        yb_ref[...] = jnp.zeros_like(yb_ref)


def _experts(chunk_e, n_used, tok3, h2r, w1, w3, w2):
    ch = MOE_CHUNK
    n_chunks = tok3.shape[0]
    smem = lambda f: pl.BlockSpec((1, 1, ch), f, memory_space=pltpu.SMEM)
    wspec = lambda s: pl.BlockSpec((None,) + s, lambda c, ce, nu: (ce[c], 0, 0))
    return pl.pallas_call(
        _expert_kernel,
        out_shape=jax.ShapeDtypeStruct((n_chunks * ch * ROW_CHUNKS, LANES), F32),
        grid_spec=pltpu.PrefetchScalarGridSpec(
            num_scalar_prefetch=2, grid=(n_chunks,),
            in_specs=[smem(lambda c, ce, nu: (c, 0, 0)),
                      smem(lambda c, ce, nu: (jnp.minimum(c + 1, n_chunks - 1), 0, 0)),
                      pl.BlockSpec(memory_space=pl.ANY),
                      wspec((D_MODEL, D_EXPERT)), wspec((D_MODEL, D_EXPERT)),
                      wspec((D_EXPERT, D_MODEL))],
            out_specs=pl.BlockSpec((ch * ROW_CHUNKS, LANES), lambda c, ce, nu: (c, 0)),
            scratch_shapes=[pltpu.VMEM((2, ch * ROW_CHUNKS, LANES), F32),
                            pltpu.SemaphoreType.DMA((2,))]),
        compiler_params=pltpu.CompilerParams(dimension_semantics=("arbitrary",),
                                             vmem_limit_bytes=VMEM_LIMIT),
        name="experts")(chunk_e, n_used, tok3, tok3, h2r, w1, w3, w2)


def _combine_kernel(dest_ref, dest_next_ref, yb_ref, x1_ref, rw_ref, o_ref, buf_ref, sem):
    i = pl.program_id(0)
    n = pl.num_programs(0)
    tb = ROUTE_ROWS
    slot = i & 1

    def gather(dests, sl):
        def issue(t, c):
            for k in range(TOP_K):
                pltpu.make_async_copy(_row(yb_ref, dests[0, 0, TOP_K * t + k]),
                                      _row(buf_ref.at[sl, k], t), sem.at[sl]).start()
            return c

        lax.fori_loop(0, tb, issue, 0, unroll=4)

    @pl.when(i == 0)
    def _():
        gather(dest_ref, 0)

    @pl.when(i + 1 < n)
    def _():
        gather(dest_next_ref, 1 - slot)

    for k in range(TOP_K):
        pltpu.make_async_copy(yb_ref.at[pl.ds(0, tb * ROW_CHUNKS), :], buf_ref.at[slot, k],
                              sem.at[slot]).wait()
    rw = rw_ref[...]
    acc = x1_ref[...]
    for k in range(TOP_K):
        y = jnp.concatenate([buf_ref[slot, k, pl.ds(s, tb, stride=ROW_CHUNKS), :]
                             for s in range(ROW_CHUNKS)], axis=-1)
        acc = acc + y * rw[:, k:k + 1]
    o_ref[...] = acc


def _combine(dest3, yb, x1, rw):
    T = x1.shape[0]
    tb = ROUTE_ROWS
    nblk = T // tb
    smem = lambda f: pl.BlockSpec((1, 1, TOP_K * tb), f, memory_space=pltpu.SMEM)
    return pl.pallas_call(
        _combine_kernel,
        out_shape=jax.ShapeDtypeStruct((T, D_MODEL), F32),
        grid=(nblk,),
        in_specs=[smem(lambda i: (i, 0, 0)), smem(lambda i: (jnp.minimum(i + 1, nblk - 1), 0, 0)),
                  pl.BlockSpec(memory_space=pl.ANY),
                  pl.BlockSpec((tb, D_MODEL), lambda i: (i, 0)),
                  pl.BlockSpec((tb, SUBLANES), lambda i: (i, 0))],
        out_specs=pl.BlockSpec((tb, D_MODEL), lambda i: (i, 0)),
        scratch_shapes=[pltpu.VMEM((2, TOP_K, tb * ROW_CHUNKS, LANES), F32),
                        pltpu.SemaphoreType.DMA((2,))],
        compiler_params=pltpu.CompilerParams(dimension_semantics=("arbitrary",),
                                             vmem_limit_bytes=VMEM_LIMIT),
        name="combine")(dest3, dest3, yb, x1, rw)


def _rope_tables(positions, dim, half_lanes, one_fill):
    inv = ROPE_THETA ** (-jnp.arange(0, dim, 2, dtype=F32) / dim)
    ang = positions.astype(F32).reshape(-1, 1) * inv
    c, s = jnp.cos(ang), jnp.sin(ang)
    T = ang.shape[0]
    fill = half_lanes - dim
    cseg = jnp.concatenate([c, c, jnp.full((T, fill), one_fill, F32)], axis=-1)
    sseg = jnp.concatenate([-s, s, jnp.zeros((T, fill), F32)], axis=-1)
    reps = LANES // half_lanes
    return jnp.tile(cseg, (1, reps)), jnp.tile(sseg, (1, reps))


def _layer(x2, B, S, tables, lambda_init, attn_norm_g, w_in, b_gate, da_q_norm_g, da_k_norm_g,
           da_lambda_q1, da_lambda_k1, da_lambda_q2, da_lambda_k2, da_subln_g, mla_q_lora_g,
           mla_w_uq, mla_kv_lora_g, mla_w_ukv, mla_q_norm_g, mla_k_nope_norm_g, mla_k_rope_norm_g,
           w_o, ffn_norm_g, w_group, b_group, w_router, b_router, w1, w3, w2):
    T = B * S
    D = D_MODEL
    c0, c1, c2 = D, 2 * D, 3 * D
    c3 = c2 + MLA_Q_LORA
    c4 = c3 + MLA_KV_LORA
    c5 = c4 + MLA_ROPE
    wb = w_in.astype(BF16)
    wkr = jnp.pad(wb[:, c4:c5], ((0, 0), (0, LANES - MLA_ROPE)))
    qk = MLA_NOPE + MLA_ROPE
    uq = mla_w_uq.astype(BF16).reshape(MLA_Q_LORA, MLA_HEADS, qk)
    wuq = jnp.concatenate([
        uq[:, :, :MLA_NOPE].reshape(MLA_Q_LORA, D),
        jnp.pad(uq[:, :, MLA_NOPE:], ((0, 0), (0, 0), (0, LANES - MLA_ROPE))).reshape(MLA_Q_LORA, D)],
        axis=1)
    ukv = mla_w_ukv.astype(BF16).reshape(MLA_KV_LORA, MLA_HEADS, MLA_NOPE + MLA_V)
    wukv = jnp.concatenate([ukv[:, :, :MLA_NOPE].reshape(MLA_KV_LORA, D),
                            ukv[:, :, MLA_NOPE:].reshape(MLA_KV_LORA, D)], axis=1)
    wts = (wb[:, :c0], wb[:, c0:c1], wb[:, c1:c2], wb[:, c2:c3], wb[:, c3:c4], wkr, wb[:, c5:],
           wuq, wukv)
    r1 = lambda v: v.astype(F32).reshape(1, -1)
    pad_rope = lambda v: jnp.pad(v.astype(F32), (0, LANES - MLA_ROPE)).reshape(1, LANES)
    da_scale = DA_HEAD_DIM ** -0.5 * LOG2_E
    mla_scale = qk ** -0.5 * LOG2_E
    vecs = (r1(b_gate),
            jnp.tile(r1(da_q_norm_g), (1, 2)) * da_scale,
            jnp.tile(r1(da_k_norm_g), (1, 2)),
            r1(mla_q_lora_g), r1(mla_kv_lora_g),
            r1(mla_q_norm_g[:MLA_NOPE]) * mla_scale,
            pad_rope(mla_q_norm_g[MLA_NOPE:]) * mla_scale,
            r1(mla_k_nope_norm_g), pad_rope(mla_k_rope_norm_g))
    q, k, v, qn, qr, kn, vm, kr, gates = _in_proj(x2, r1(attn_norm_g), wts, vecs, tables)

    b3 = lambda a: a.reshape(B, S, a.shape[-1])
    lam_vecs = jnp.stack([da_lambda_q1, da_lambda_k1, da_lambda_q2, da_lambda_k2]).astype(F32)
    o_da = _da_attn(b3(q), b3(k), b3(v), lam_vecs, r1(da_subln_g), lambda_init)
    o_mla = _mla_attn(b3(qn), b3(qr), b3(kn), b3(kr), b3(vm))

    wgr = jnp.concatenate([w_group.astype(F32), w_router.astype(F32),
                           jnp.zeros((D, LANES - N_GROUPS - N_EXPERTS), F32)], axis=1)
    whi = wgr.astype(BF16)
    wlo = (wgr - whi.astype(F32)).astype(BF16)
    br = jnp.concatenate([b_group.astype(F32), b_router.astype(F32),
                          jnp.zeros((LANES - N_GROUPS - N_EXPERTS,), F32)]).reshape(1, LANES)
    x1, h2r, ri, rw, cnt = _merge(o_da.reshape(T, D), o_mla.reshape(T, D), gates, x2,
                                  w_o.astype(BF16), r1(ffn_norm_g), whi, wlo, br)

    ch = MOE_CHUNK
    counts = cnt[0, :N_EXPERTS].astype(jnp.int32)
    padded = (counts + ch - 1) // ch * ch
    pends = jnp.cumsum(padded)
    pstarts = pends - padded
    n_chunks = (T * TOP_K) // ch + N_EXPERTS
    P = n_chunks * ch
    dest = pstarts[ri[:, :TOP_K]] + ri[:, TOP_K:2 * TOP_K]
    dest3 = dest.reshape(T // ROUTE_ROWS, 1, TOP_K * ROUTE_ROWS)
    chunk_start = jnp.arange(n_chunks, dtype=jnp.int32)[:, None] * ch
    chunk_e = jnp.minimum(jnp.sum((pends[None, :] <= chunk_start).astype(jnp.int32), axis=1),
                          N_EXPERTS - 1).astype(jnp.int32)
    n_used = (pends[-1:] // ch).astype(jnp.int32)
    tok = jnp.repeat(jnp.arange(T, dtype=jnp.int32), TOP_K)
    tok3 = jnp.zeros((P,), jnp.int32).at[dest.reshape(-1)].set(
        tok, unique_indices=True, mode='promise_in_bounds').reshape(n_chunks, 1, ch)

    yb = _experts(chunk_e, n_used, tok3, h2r, w1.astype(BF16), w3.astype(BF16), w2.astype(BF16))
    return _combine(dest3, yb, x1, rw)


def kernel(x, positions, attn_norm_g, w_in, b_gate, da_q_norm_g, da_k_norm_g, da_lambda_q1, da_lambda_k1, da_lambda_q2, da_lambda_k2, da_subln_g, mla_q_lora_g, mla_w_uq, mla_kv_lora_g, mla_w_ukv, mla_q_norm_g, mla_k_nope_norm_g, mla_k_rope_norm_g, w_o, ffn_norm_g, w_group, b_group, w_router, b_router, w1, w3, w2):
    B, S, D = x.shape
    assert D == D_MODEL and S % MLA_TQ == 0 and (B * S) % PROJ_ROWS == 0
    tables = (_rope_tables(positions, DA_ROT_DIM, DA_HEAD_DIM, 1.0)
              + _rope_tables(positions, MLA_ROPE, LANES, 1.0))
    x2 = x.reshape(B * S, D)
    per_layer = (attn_norm_g, w_in, b_gate, da_q_norm_g, da_k_norm_g, da_lambda_q1, da_lambda_k1,
                 da_lambda_q2, da_lambda_k2, da_subln_g, mla_q_lora_g, mla_w_uq, mla_kv_lora_g,
                 mla_w_ukv, mla_q_norm_g, mla_k_nope_norm_g, mla_k_rope_norm_g, w_o, ffn_norm_g,
                 w_group, b_group, w_router, b_router, w1, w3, w2)
    for l in range(w_in.shape[0]):
        lambda_init = 0.8 - 0.6 * math.exp(-0.3 * l)
        x2 = _layer(x2, B, S, tables, lambda_init, *[p[l] for p in per_layer])
    return x2.reshape(B, S, D)
```

```python
import functools
import math

import jax
import jax.numpy as jnp
from jax import lax
from jax.experimental import pallas as pl
from jax.experimental.pallas import tpu as pltpu

D_MODEL = 1024
DA_HEADS = 8
DA_HEAD_DIM = 64
DA_V_DIM = 128
DA_ROT_DIM = 16
MLA_HEADS = 8
MLA_Q_LORA = 384
MLA_KV_LORA = 256
MLA_NOPE = 128
MLA_ROPE = 64
MLA_V = 128
ROPE_THETA = 500000.0
N_GROUPS = 4
EXPERTS_PER_GROUP = 8
N_EXPERTS = N_GROUPS * EXPERTS_PER_GROUP
TOP_K = 2
D_EXPERT = 512
NORM_EPS = 1e-6
LOG2_E = math.log2(math.e)

LANES = 128
SUBLANES = 8
ROW_CHUNKS = D_MODEL // LANES

INPROJ_ROWS = 512
PROJ_ROWS = 256
DA_TQ = 512
MLA_TQ = 512
ATT_TK = 512
MOE_CHUNK = 256
ROUTE_ROWS = 256
VMEM_LIMIT = 48 * 1024 * 1024
INPROJ_VMEM_LIMIT = 56 * 1024 * 1024

F32 = jnp.float32
BF16 = jnp.bfloat16


def _const_spec(shape):
    nd = len(shape)
    return pl.BlockSpec(shape, lambda *_: (0,) * nd, pipeline_mode=pl.Buffered(1))


def _rms(v, n):
    return lax.rsqrt(jnp.sum(v * v, axis=-1, keepdims=True) * (1.0 / n) + NORM_EPS)


def _dot(a, b):
    return jnp.dot(a, b, preferred_element_type=F32)


def _dot_nt(a, b):
    return lax.dot_general(a, b, (((1,), (1,)), ((), ())), preferred_element_type=F32)


def _in_proj_kernel(x_ref, g_ref, wq_ref, wk_ref, wv_ref, wcq_ref, wckv_ref, wkr_ref, wg_ref,
                    wuq_ref, wukv_ref, bg_ref, qg_ref, kg_ref, qlg_ref, kvlg_ref, mqn_ref, mqr_ref,
                    mkn_ref, mkr_ref, cda_ref, sda_ref, cm_ref, sm_ref,
                    q_ref, k_ref, v_ref, qn_ref, qr_ref, kn_ref, vm_ref, kr_ref, gate_ref):
    x = x_ref[...]
    h = (x * _rms(x, D_MODEL) * g_ref[...]).astype(BF16)
    rows = x.shape[0]
    lane = lax.broadcasted_iota(jnp.int32, (rows, LANES), 1)
    lo = lane < DA_HEAD_DIM
    da_first = (lane & (DA_HEAD_DIM - 1)) < (DA_ROT_DIM // 2)
    m_first = lane < (MLA_ROPE // 2)
    cda, sda = cda_ref[...], sda_ref[...]
    cm, sm = cm_ref[...], sm_ref[...]

    def da_rope(y):
        partner = jnp.where(da_first, pltpu.roll(y, LANES - DA_ROT_DIM // 2, 1),
                            pltpu.roll(y, DA_ROT_DIM // 2, 1))
        return y * cda + partner * sda

    def mla_rope(y):
        partner = jnp.where(m_first, pltpu.roll(y, LANES - MLA_ROPE // 2, 1),
                            pltpu.roll(y, MLA_ROPE // 2, 1))
        return y * cm + partner * sm

    for w_ref, gain_ref, o_ref in ((wq_ref, qg_ref, q_ref), (wk_ref, kg_ref, k_ref)):
        p_all = _dot(h, w_ref[...])
        gain = gain_ref[...]
        for hd in range(DA_HEADS):
            sl = slice(hd * LANES, (hd + 1) * LANES)
            p = p_all[:, sl]
            p2 = p * p
            s_all = jnp.sum(p2, axis=-1, keepdims=True)
            s_lo = jnp.sum(jnp.where(lo, p2, 0.0), axis=-1, keepdims=True)
            inv = jnp.where(lo, lax.rsqrt(s_lo * (1.0 / DA_HEAD_DIM) + NORM_EPS),
                            lax.rsqrt((s_all - s_lo) * (1.0 / DA_HEAD_DIM) + NORM_EPS))
            o_ref[:, sl] = da_rope(p * inv * gain).astype(BF16)
    v_ref[...] = _dot(h, wv_ref[...]).astype(BF16)

    cq = _dot(h, wcq_ref[...])
    cqn = (cq * _rms(cq, MLA_Q_LORA) * qlg_ref[...]).astype(BF16)
    qm = _dot(cqn, wuq_ref[...])
    gn, gr = mqn_ref[...], mqr_ref[...]
    for hd in range(MLA_HEADS):
        sl = slice(hd * LANES, (hd + 1) * LANES)
        qn = qm[:, sl]
        qr = qm[:, D_MODEL + hd * LANES:D_MODEL + (hd + 1) * LANES]
        ss = jnp.sum(qn * qn, axis=-1, keepdims=True) + jnp.sum(qr * qr, axis=-1, keepdims=True)
        inv = lax.rsqrt(ss * (1.0 / (MLA_NOPE + MLA_ROPE)) + NORM_EPS)
        qn_ref[:, sl] = (qn * inv * gn).astype(BF16)
        qr_ref[:, sl] = mla_rope(qr * inv * gr).astype(BF16)

    ckv = _dot(h, wckv_ref[...])
    ckvn = (ckv * _rms(ckv, MLA_KV_LORA) * kvlg_ref[...]).astype(BF16)
    kv = _dot(ckvn, wukv_ref[...])
    gkn = mkn_ref[...]
    for hd in range(MLA_HEADS):
        sl = slice(hd * LANES, (hd + 1) * LANES)
        kn = kv[:, sl]
        kn_ref[:, sl] = (kn * _rms(kn, MLA_NOPE) * gkn).astype(BF16)
    vm_ref[...] = kv[:, D_MODEL:].astype(BF16)
    kr = _dot(h, wkr_ref[...])
    kr_ref[...] = mla_rope(kr * _rms(kr, MLA_ROPE) * mkr_ref[...]).astype(BF16)

    gl = _dot(h, wg_ref[...]) + bg_ref[...]
    gate_ref[...] = (1.0 / (1.0 + jnp.exp(-gl))).astype(BF16)


def _in_proj(x2, g, wts, vecs, tables):
    T = x2.shape[0]
    tm = INPROJ_ROWS
    row = lambda w: pl.BlockSpec((tm, w), lambda i: (i, 0))
    in_specs = ([row(D_MODEL), _const_spec(g.shape)] + [_const_spec(w.shape) for w in wts]
                + [_const_spec(v.shape) for v in vecs] + [row(LANES)] * 4)
    outs = [jax.ShapeDtypeStruct((T, D_MODEL), BF16)] * 7 + [
        jax.ShapeDtypeStruct((T, LANES), BF16), jax.ShapeDtypeStruct((T, 2 * D_MODEL), BF16)]
    out_specs = [row(D_MODEL)] * 7 + [row(LANES), row(2 * D_MODEL)]
    return pl.pallas_call(
        _in_proj_kernel, out_shape=outs, grid=(T // tm,), in_specs=in_specs, out_specs=out_specs,
        compiler_params=pltpu.CompilerParams(dimension_semantics=("parallel",),
                                             vmem_limit_bytes=INPROJ_VMEM_LIMIT),
        name="in_proj")(x2, g, *wts, *vecs, *tables)


def _flash(qs, load_k, load_v, qi, tq, n_maps, sa_ref, sb_ref, m_ref, l_ref, acc_ref):
    rows = n_maps * tq
    tk = ATT_TK
    assert tk % tq == 0
    m_ref[...] = jnp.full(m_ref.shape, -jnp.inf, F32)
    l_ref[...] = jnp.zeros(l_ref.shape, F32)
    acc_ref[...] = jnp.zeros(acc_ref.shape, F32)

    def scores(j, s_ref):
        s_ref[...] = _dot_nt(qs, load_k(j))

    def update(j, s_ref, masked):
        s = s_ref[...]
        if masked:
            qpos = qi * tq + (lax.broadcasted_iota(jnp.int32, (rows, tk), 0) & (tq - 1))
            kpos = j * tk + lax.broadcasted_iota(jnp.int32, (rows, tk), 1)
            s = jnp.where(kpos <= qpos, s, -jnp.inf)
        m_old = m_ref[...]
        m_new = jnp.maximum(m_old, jnp.max(s, axis=-1, keepdims=True))
        alpha = jnp.exp2(m_old - m_new)
        p = jnp.exp2(s - m_new)
        psum = p[:, :LANES]
        for c in range(1, tk // LANES):
            psum = psum + p[:, c * LANES:(c + 1) * LANES]
        l_ref[...] = alpha * l_ref[...] + psum
        acc_ref[...] = alpha * acc_ref[...] + _dot(p.astype(BF16), load_v(j))
        m_ref[...] = m_new

    n_full = (qi * tq) // tk
    n_pairs = n_full // 2
    tail2 = n_full - 2 * n_pairs

    scores(0, sa_ref)

    def body(i, carry):
        scores(2 * i + 1, sb_ref)
        update(2 * i, sa_ref, False)
        scores(2 * i + 2, sa_ref)
        update(2 * i + 1, sb_ref, False)
        return carry

    lax.fori_loop(0, n_pairs, body, 0)
    e0 = 2 * n_pairs

    @pl.when(tail2 == 1)
    def _():
        scores(e0 + 1, sb_ref)
        update(e0, sa_ref, False)
        update(e0 + 1, sb_ref, True)

    @pl.when(tail2 == 0)
    def _():
        update(e0, sa_ref, True)

    return acc_ref[...], jnp.sum(l_ref[...], axis=-1, keepdims=True)


def _flash_scratch(rows):
    return [pltpu.VMEM((rows, ATT_TK), F32), pltpu.VMEM((rows, ATT_TK), F32),
            pltpu.VMEM((rows, 1), F32), pltpu.VMEM((rows, LANES), F32),
            pltpu.VMEM((rows, LANES), F32)]


def _kv_block(j):
    return pl.ds(pl.multiple_of(j * ATT_TK, ATT_TK), ATT_TK)


def _da_attn_kernel(q_ref, k_ref, v_ref, lam_ref, sg_ref, o_ref, *scratch, lambda_init):
    qi = pl.program_id(2)
    tq = DA_TQ
    q = q_ref[...]
    lane = lax.broadcasted_iota(jnp.int32, q.shape, 1)
    zero = jnp.zeros_like(q)
    qs = jnp.concatenate([jnp.where(lane < DA_HEAD_DIM, q, zero),
                          jnp.where(lane >= DA_HEAD_DIM, q, zero)], axis=0)

    acc, l = _flash(qs, lambda j: k_ref[_kv_block(j), :], lambda j: v_ref[_kv_block(j), :],
                    qi, tq, 2, *scratch)
    o1 = acc[:tq] / l[:tq]
    o2 = acc[tq:] / l[tq:]
    lv = lam_ref[...]
    lam = (jnp.exp(jnp.sum(lv[0:1] * lv[1:2], axis=-1, keepdims=True))
           - jnp.exp(jnp.sum(lv[2:3] * lv[3:4], axis=-1, keepdims=True)) + lambda_init)
    o = o1 - lam * o2
    y = o * _rms(o, DA_V_DIM) * sg_ref[...] * (1.0 - lambda_init)
    o_ref[...] = y.astype(BF16)


def _da_attn(q, k, v, lam_vecs, subln_g, lambda_init):
    B, S, _ = q.shape
    tq = DA_TQ
    qspec = pl.BlockSpec((None, tq, LANES), lambda b, h, i: (b, i, h))
    kvspec = pl.BlockSpec((None, S, LANES), lambda b, h, i: (b, 0, h))
    return pl.pallas_call(
        functools.partial(_da_attn_kernel, lambda_init=lambda_init),
        out_shape=jax.ShapeDtypeStruct((B, S, D_MODEL), BF16),
        grid=(B, DA_HEADS, S // tq),
        in_specs=[qspec, kvspec, kvspec, _const_spec(lam_vecs.shape), _const_spec(subln_g.shape)],
        out_specs=qspec,
        scratch_shapes=_flash_scratch(2 * tq),
        compiler_params=pltpu.CompilerParams(
            dimension_semantics=("parallel", "parallel", "arbitrary"), vmem_limit_bytes=VMEM_LIMIT),
        name="da_attn")(q, k, v, lam_vecs, subln_g)


def _mla_attn_kernel(qn_ref, qr_ref, kn_ref, kr_ref, v_ref, o_ref, *scratch):
    qi = pl.program_id(2)
    qs = jnp.concatenate([qn_ref[...], qr_ref[...]], axis=-1)

    def load_k(j):
        return jnp.concatenate([kn_ref[_kv_block(j), :], kr_ref[_kv_block(j), :]], axis=-1)

    acc, l = _flash(qs, load_k, lambda j: v_ref[_kv_block(j), :], qi, MLA_TQ, 1, *scratch)
    o_ref[...] = (acc / l).astype(BF16)


def _mla_attn(qn, qr, kn, kr, vm):
    B, S, _ = qn.shape
    tq = MLA_TQ
    qspec = pl.BlockSpec((None, tq, LANES), lambda b, h, i: (b, i, h))
    kvspec = pl.BlockSpec((None, S, LANES), lambda b, h, i: (b, 0, h))
    krspec = pl.BlockSpec((None, S, LANES), lambda b, h, i: (b, 0, 0))
    return pl.pallas_call(
        _mla_attn_kernel,
        out_shape=jax.ShapeDtypeStruct((B, S, D_MODEL), BF16),
        grid=(B, MLA_HEADS, S // tq),
        in_specs=[qspec, qspec, kvspec, krspec, kvspec],
        out_specs=qspec,
        scratch_shapes=_flash_scratch(tq),
        compiler_params=pltpu.CompilerParams(
            dimension_semantics=("parallel", "parallel", "arbitrary"), vmem_limit_bytes=VMEM_LIMIT),
        name="mla_attn")(qn, qr, kn, kr, vm)


def _merge_kernel(oda_ref, omla_ref, gate_ref, x_ref, wo_ref, fg_ref, whi_ref, wlo_ref, br_ref,
                  x1_ref, h2_ref, ri_ref, rw_ref, cnt_ref, carry_ref):
    i = pl.program_id(0)
    rows = x_ref.shape[0]

    @pl.when(i == 0)
    def _():
        carry_ref[...] = jnp.zeros_like(carry_ref)

    g = gate_ref[...].astype(F32)
    mixed = g[:, :D_MODEL] * oda_ref[...].astype(F32) + g[:, D_MODEL:] * omla_ref[...].astype(F32)
    x1 = x_ref[...] + _dot(mixed.astype(BF16), wo_ref[...])
    x1_ref[...] = x1
    h2 = x1 * _rms(x1, D_MODEL) * fg_ref[...]
    for s in range(ROW_CHUNKS):
        h2_ref[pl.ds(s, rows, stride=ROW_CHUNKS), :] = h2[:, s * LANES:(s + 1) * LANES]

    hi = h2.astype(BF16)
    lo = (h2 - hi.astype(F32)).astype(BF16)
    whi = whi_ref[...]
    logits = _dot(hi, whi) + _dot(hi, wlo_ref[...]) + _dot(lo, whi) + br_ref[...]

    lane = lax.broadcasted_iota(jnp.int32, (rows, LANES), 1)
    neg = -jnp.inf
    gl = jnp.where(lane < N_GROUPS, logits, neg)
    gmax = jnp.max(gl, axis=-1, keepdims=True)
    g_sel = jnp.min(jnp.where(gl == gmax, lane, LANES), axis=-1, keepdims=True)
    p_g = 1.0 / jnp.sum(jnp.exp(gl - gmax), axis=-1, keepdims=True)
    e_lo = N_GROUPS + g_sel * EXPERTS_PER_GROUP
    el = jnp.where((lane >= e_lo) & (lane < e_lo + EXPERTS_PER_GROUP), logits, neg)
    v0 = jnp.max(el, axis=-1, keepdims=True)
    i0 = jnp.min(jnp.where(el == v0, lane, LANES), axis=-1, keepdims=True)
    el2 = jnp.where(lane == i0, neg, el)
    v1 = jnp.max(el2, axis=-1, keepdims=True)
    i1 = jnp.min(jnp.where(el2 == v1, lane, LANES), axis=-1, keepdims=True)
    t = jnp.exp(v1 - v0)
    w0 = p_g / (1.0 + t)
    w1 = p_g * t / (1.0 + t)
    e0 = i0 - N_GROUPS
    e1 = i1 - N_GROUPS

    oh = jnp.where((lane == e0) | (lane == e1), 1.0, 0.0)
    r_i = lax.broadcasted_iota(jnp.int32, (rows, rows), 0)
    c_i = lax.broadcasted_iota(jnp.int32, (rows, rows), 1)
    tri = jnp.where(c_i < r_i, 1.0, 0.0).astype(BF16)
    before = _dot(tri, oh.astype(BF16)) + carry_ref[...]
    rank0 = jnp.sum(jnp.where(lane == e0, before, 0.0), axis=-1, keepdims=True).astype(jnp.int32)
    rank1 = jnp.sum(jnp.where(lane == e1, before, 0.0), axis=-1, keepdims=True).astype(jnp.int32)
    carry = carry_ref[...] + jnp.sum(oh, axis=0, keepdims=True)
    carry_ref[...] = carry
    cnt_ref[...] = carry

    ri = jnp.where(lane == 0, e0, jnp.where(lane == 1, e1, jnp.where(lane == 2, rank0, rank1)))
    ri_ref[...] = ri[:, :SUBLANES]
    rw_ref[...] = jnp.where(lane == 0, w0, w1)[:, :SUBLANES]


def _merge(oda, omla, gates, x2, wo, fg, whi, wlo, br):
    T = x2.shape[0]
    tm = PROJ_ROWS
    row = lambda w: pl.BlockSpec((tm, w), lambda i: (i, 0))
    outs = [jax.ShapeDtypeStruct((T, D_MODEL), F32),
            jax.ShapeDtypeStruct((T * ROW_CHUNKS, LANES), F32),
            jax.ShapeDtypeStruct((T, SUBLANES), jnp.int32),
            jax.ShapeDtypeStruct((T, SUBLANES), F32),
            jax.ShapeDtypeStruct((1, LANES), F32)]
    out_specs = [row(D_MODEL), pl.BlockSpec((tm * ROW_CHUNKS, LANES), lambda i: (i, 0)),
                 row(SUBLANES), row(SUBLANES), pl.BlockSpec((1, LANES), lambda i: (0, 0))]
    return pl.pallas_call(
        _merge_kernel, out_shape=outs, grid=(T // tm,),
        in_specs=[row(D_MODEL), row(D_MODEL), row(2 * D_MODEL), row(D_MODEL), _const_spec(wo.shape),
                  _const_spec(fg.shape), _const_spec(whi.shape), _const_spec(wlo.shape),
                  _const_spec(br.shape)],
        out_specs=out_specs,
        scratch_shapes=[pltpu.VMEM((1, LANES), F32)],
        compiler_params=pltpu.CompilerParams(dimension_semantics=("arbitrary",),
                                             vmem_limit_bytes=VMEM_LIMIT),
        name="merge_router")(oda, omla, gates, x2, wo, fg, whi, wlo, br)


def _row(ref, r):
    return ref.at[pl.ds(pl.multiple_of(r * ROW_CHUNKS, ROW_CHUNKS), ROW_CHUNKS), :]


def _expert_kernel(ce_ref, nu_ref, tok_ref, tok_next_ref, h2_ref, w1_ref, w3_ref, w2_ref, yb_ref,
                   buf_ref, sem):
    c = pl.program_id(0)
    ch = MOE_CHUNK
    slot = c & 1
    n_used = nu_ref[0]

    def row_copy(toks, sl, t):
        return pltpu.make_async_copy(_row(h2_ref, toks[0, 0, t]), _row(buf_ref.at[sl], t),
                                     sem.at[sl])

    def slot_wait(sl):
        pltpu.make_async_copy(h2_ref.at[pl.ds(0, ch * ROW_CHUNKS), :], buf_ref.at[sl],
                              sem.at[sl]).wait()

    @pl.when(c == 0)
    def _():
        def issue(t, carry):
            row_copy(tok_ref, 0, t).start()
            return carry

        lax.fori_loop(0, ch, issue, 0, unroll=8)

    @pl.when(c < n_used)
    def _():
        slot_wait(slot)
        for t in range(ch):
            row_copy(tok_next_ref, 1 - slot, t).start()
        x = jnp.concatenate([buf_ref[slot, pl.ds(s, ch, stride=ROW_CHUNKS), :]
                             for s in range(ROW_CHUNKS)], axis=-1).astype(BF16)
        a = _dot(x, w1_ref[...])
        b = _dot(x, w3_ref[...])
        hmid = (a / (1.0 + jnp.exp(-a))) * b
        y = _dot(hmid.astype(BF16), w2_ref[...])
        for s in range(ROW_CHUNKS):
            yb_ref[pl.ds(s, ch, stride=ROW_CHUNKS), :] = y[:, s * LANES:(s + 1) * LANES]

    @pl.when(c >= n_used)
    def _():
        yb_ref[...] = jnp.zeros_like(yb_ref)

    @pl.when(c == n_used)
    def _():
        slot_wait(slot)


def _experts(chunk_e, n_used, tok3, h2r, w1, w3, w2):
    ch = MOE_CHUNK
    n_chunks = tok3.shape[0]
    smem = lambda f: pl.BlockSpec((1, 1, ch), f, memory_space=pltpu.SMEM)
    wspec = lambda s: pl.BlockSpec((None,) + s, lambda c, ce, nu: (ce[c], 0, 0))
    return pl.pallas_call(
        _expert_kernel,
        out_shape=jax.ShapeDtypeStruct((n_chunks * ch * ROW_CHUNKS, LANES), F32),
        grid_spec=pltpu.PrefetchScalarGridSpec(
            num_scalar_prefetch=2, grid=(n_chunks,),
            in_specs=[smem(lambda c, ce, nu: (c, 0, 0)),
                      smem(lambda c, ce, nu: (jnp.minimum(c + 1, n_chunks - 1), 0, 0)),
                      pl.BlockSpec(memory_space=pl.ANY),
                      wspec((D_MODEL, D_EXPERT)), wspec((D_MODEL, D_EXPERT)),
                      wspec((D_EXPERT, D_MODEL))],
            out_specs=pl.BlockSpec((ch * ROW_CHUNKS, LANES), lambda c, ce, nu: (c, 0)),
            scratch_shapes=[pltpu.VMEM((2, ch * ROW_CHUNKS, LANES), F32),
                            pltpu.SemaphoreType.DMA((2,))]),
        compiler_params=pltpu.CompilerParams(dimension_semantics=("arbitrary",),
                                             vmem_limit_bytes=VMEM_LIMIT),
        name="experts")(chunk_e, n_used, tok3, tok3, h2r, w1, w3, w2)


def _combine_kernel(dest_ref, dest_next_ref, yb_ref, x1_ref, rw_ref, o_ref, buf_ref, sem):
    i = pl.program_id(0)
    n = pl.num_programs(0)
    tb = ROUTE_ROWS
    slot = i & 1

    def gather(dests, sl):
        def issue(t, c):
            for k in range(TOP_K):
                pltpu.make_async_copy(_row(yb_ref, dests[0, 0, TOP_K * t + k]),
                                      _row(buf_ref.at[sl, k], t), sem.at[sl]).start()
            return c

        lax.fori_loop(0, tb, issue, 0, unroll=4)

    @pl.when(i == 0)
    def _():
        gather(dest_ref, 0)

    @pl.when(i + 1 < n)
    def _():
        gather(dest_next_ref, 1 - slot)

    for k in range(TOP_K):
        pltpu.make_async_copy(yb_ref.at[pl.ds(0, tb * ROW_CHUNKS), :], buf_ref.at[slot, k],
                              sem.at[slot]).wait()
    rw = rw_ref[...]
    acc = x1_ref[...]
    for k in range(TOP_K):
        y = jnp.concatenate([buf_ref[slot, k, pl.ds(s, tb, stride=ROW_CHUNKS), :]
                             for s in range(ROW_CHUNKS)], axis=-1)
        acc = acc + y * rw[:, k:k + 1]
    o_ref[...] = acc


def _combine(dest3, yb, x1, rw):
    T = x1.shape[0]
    tb = ROUTE_ROWS
    nblk = T // tb
    smem = lambda f: pl.BlockSpec((1, 1, TOP_K * tb), f, memory_space=pltpu.SMEM)
    return pl.pallas_call(
        _combine_kernel,
        out_shape=jax.ShapeDtypeStruct((T, D_MODEL), F32),
        grid=(nblk,),
        in_specs=[smem(lambda i: (i, 0, 0)), smem(lambda i: (jnp.minimum(i + 1, nblk - 1), 0, 0)),
                  pl.BlockSpec(memory_space=pl.ANY),
                  pl.BlockSpec((tb, D_MODEL), lambda i: (i, 0)),
                  pl.BlockSpec((tb, SUBLANES), lambda i: (i, 0))],
        out_specs=pl.BlockSpec((tb, D_MODEL), lambda i: (i, 0)),
        scratch_shapes=[pltpu.VMEM((2, TOP_K, tb * ROW_CHUNKS, LANES), F32),
                        pltpu.SemaphoreType.DMA((2,))],
        compiler_params=pltpu.CompilerParams(dimension_semantics=("arbitrary",),
                                             vmem_limit_bytes=VMEM_LIMIT),
        name="combine")(dest3, dest3, yb, x1, rw)


def _rope_tables(positions, dim, half_lanes, one_fill):
    inv = ROPE_THETA ** (-jnp.arange(0, dim, 2, dtype=F32) / dim)
    ang = positions.astype(F32).reshape(-1, 1) * inv
    c, s = jnp.cos(ang), jnp.sin(ang)
    T = ang.shape[0]
    fill = half_lanes - dim
    cseg = jnp.concatenate([c, c, jnp.full((T, fill), one_fill, F32)], axis=-1)
    sseg = jnp.concatenate([-s, s, jnp.zeros((T, fill), F32)], axis=-1)
    reps = LANES // half_lanes
    return jnp.tile(cseg, (1, reps)), jnp.tile(sseg, (1, reps))


def _layer(x2, B, S, tables, lambda_init, attn_norm_g, w_in, b_gate, da_q_norm_g, da_k_norm_g,
           da_lambda_q1, da_lambda_k1, da_lambda_q2, da_lambda_k2, da_subln_g, mla_q_lora_g,
           mla_w_uq, mla_kv_lora_g, mla_w_ukv, mla_q_norm_g, mla_k_nope_norm_g, mla_k_rope_norm_g,
           w_o, ffn_norm_g, w_group, b_group, w_router, b_router, w1, w3, w2):
    T = B * S
    D = D_MODEL
    c0, c1, c2 = D, 2 * D, 3 * D
    c3 = c2 + MLA_Q_LORA
    c4 = c3 + MLA_KV_LORA
    c5 = c4 + MLA_ROPE
    wb = w_in.astype(BF16)
    wkr = jnp.pad(wb[:, c4:c5], ((0, 0), (0, LANES - MLA_ROPE)))
    qk = MLA_NOPE + MLA_ROPE
    uq = mla_w_uq.astype(BF16).reshape(MLA_Q_LORA, MLA_HEADS, qk)
    wuq = jnp.concatenate([
        uq[:, :, :MLA_NOPE].reshape(MLA_Q_LORA, D),
        jnp.pad(uq[:, :, MLA_NOPE:], ((0, 0), (0, 0), (0, LANES - MLA_ROPE))).reshape(MLA_Q_LORA, D)],
        axis=1)
    ukv = mla_w_ukv.astype(BF16).reshape(MLA_KV_LORA, MLA_HEADS, MLA_NOPE + MLA_V)
    wukv = jnp.concatenate([ukv[:, :, :MLA_NOPE].reshape(MLA_KV_LORA, D),
                            ukv[:, :, MLA_NOPE:].reshape(MLA_KV_LORA, D)], axis=1)
    wts = (wb[:, :c0], wb[:, c0:c1], wb[:, c1:c2], wb[:, c2:c3], wb[:, c3:c4], wkr, wb[:, c5:],
           wuq, wukv)
    r1 = lambda v: v.astype(F32).reshape(1, -1)
    pad_rope = lambda v: jnp.pad(v.astype(F32), (0, LANES - MLA_ROPE)).reshape(1, LANES)
    da_scale = DA_HEAD_DIM ** -0.5 * LOG2_E
    mla_scale = qk ** -0.5 * LOG2_E
    vecs = (r1(b_gate),
            jnp.tile(r1(da_q_norm_g), (1, 2)) * da_scale,
            jnp.tile(r1(da_k_norm_g), (1, 2)),
            r1(mla_q_lora_g), r1(mla_kv_lora_g),
            r1(mla_q_norm_g[:MLA_NOPE]) * mla_scale,
            pad_rope(mla_q_norm_g[MLA_NOPE:]) * mla_scale,
            r1(mla_k_nope_norm_g), pad_rope(mla_k_rope_norm_g))
    q, k, v, qn, qr, kn, vm, kr, gates = _in_proj(x2, r1(attn_norm_g), wts, vecs, tables)

    b3 = lambda a: a.reshape(B, S, a.shape[-1])
    lam_vecs = jnp.stack([da_lambda_q1, da_lambda_k1, da_lambda_q2, da_lambda_k2]).astype(F32)
    o_da = _da_attn(b3(q), b3(k), b3(v), lam_vecs, r1(da_subln_g), lambda_init)
    o_mla = _mla_attn(b3(qn), b3(qr), b3(kn), b3(kr), b3(vm))

    wgr = jnp.concatenate([w_group.astype(F32), w_router.astype(F32),
                           jnp.zeros((D, LANES - N_GROUPS - N_EXPERTS), F32)], axis=1)
    whi = wgr.astype(BF16)
    wlo = (wgr - whi.astype(F32)).astype(BF16)
    br = jnp.concatenate([b_group.astype(F32), b_router.astype(F32),
                          jnp.zeros((LANES - N_GROUPS - N_EXPERTS,), F32)]).reshape(1, LANES)
    x1, h2r, ri, rw, cnt = _merge(o_da.reshape(T, D), o_mla.reshape(T, D), gates, x2,
                                  w_o.astype(BF16), r1(ffn_norm_g), whi, wlo, br)

    ch = MOE_CHUNK
    counts = cnt[0, :N_EXPERTS].astype(jnp.int32)
    padded = (counts + ch - 1) // ch * ch
    pends = jnp.cumsum(padded)
    pstarts = pends - padded
    n_chunks = (T * TOP_K) // ch + N_EXPERTS
    P = n_chunks * ch
    e_ids = jnp.arange(N_EXPERTS, dtype=jnp.int32)
    seg_start = jnp.sum(jnp.where(ri[:, :TOP_K, None] == e_ids, pstarts, 0), axis=-1)
    dest = seg_start + ri[:, TOP_K:2 * TOP_K]
    dest3 = dest.reshape(T // ROUTE_ROWS, 1, TOP_K * ROUTE_ROWS)
    chunk_start = jnp.arange(n_chunks, dtype=jnp.int32)[:, None] * ch
    chunk_e = jnp.minimum(jnp.sum((pends[None, :] <= chunk_start).astype(jnp.int32), axis=1),
                          N_EXPERTS - 1).astype(jnp.int32)
    n_used = (pends[-1:] // ch).astype(jnp.int32)
    tok = jnp.repeat(jnp.arange(T, dtype=jnp.int32), TOP_K)
    tok3 = jnp.zeros((P,), jnp.int32).at[dest.reshape(-1)].set(
        tok, unique_indices=True, mode='promise_in_bounds').reshape(n_chunks, 1, ch)

    yb = _experts(chunk_e, n_used, tok3, h2r, w1.astype(BF16), w3.astype(BF16), w2.astype(BF16))
    return _combine(dest3, yb, x1, rw)


def kernel(x, positions, attn_norm_g, w_in, b_gate, da_q_norm_g, da_k_norm_g, da_lambda_q1, da_lambda_k1, da_lambda_q2, da_lambda_k2, da_subln_g, mla_q_lora_g, mla_w_uq, mla_kv_lora_g, mla_w_ukv, mla_q_norm_g, mla_k_nope_norm_g, mla_k_rope_norm_g, w_o, ffn_norm_g, w_group, b_group, w_router, b_router, w1, w3, w2):
    B, S, D = x.shape
    assert D == D_MODEL and S % max(DA_TQ, MLA_TQ, ATT_TK) == 0 and (B * S) % INPROJ_ROWS == 0
    tables = (_rope_tables(positions, DA_ROT_DIM, DA_HEAD_DIM, 1.0)
              + _rope_tables(positions, MLA_ROPE, LANES, 1.0))
    x2 = x.reshape(B * S, D)
    per_layer = (attn_norm_g, w_in, b_gate, da_q_norm_g, da_k_norm_g, da_lambda_q1, da_lambda_k1,
                 da_lambda_q2, da_lambda_k2, da_subln_g, mla_q_lora_g, mla_w_uq, mla_kv_lora_g,
                 mla_w_ukv, mla_q_norm_g, mla_k_nope_norm_g, mla_k_rope_norm_g, w_o, ffn_norm_g,
                 w_group, b_group, w_router, b_router, w1, w3, w2)
    for l in range(w_in.shape[0]):
        lambda_init = 0.8 - 0.6 * math.exp(-0.3 * l)
        x2 = _layer(x2, B, S, tables, lambda_init, *[p[l] for p in per_layer])
    return x2.reshape(B, S, D)
```

```python
import functools
import math

import jax
import jax.numpy as jnp
from jax import lax
from jax.experimental import pallas as pl
from jax.experimental.pallas import tpu as pltpu

D_MODEL = 1024
DA_HEADS = 8
DA_HEAD_DIM = 64
DA_V_DIM = 128
DA_ROT_DIM = 16
MLA_HEADS = 8
MLA_Q_LORA = 384
MLA_KV_LORA = 256
MLA_NOPE = 128
MLA_ROPE = 64
MLA_V = 128
ROPE_THETA = 500000.0
N_GROUPS = 4
EXPERTS_PER_GROUP = 8
N_EXPERTS = N_GROUPS * EXPERTS_PER_GROUP
TOP_K = 2
D_EXPERT = 512
NORM_EPS = 1e-6
LOG2_E = math.log2(math.e)

LANES = 128
SUBLANES = 8
ROW_CHUNKS = D_MODEL // LANES

INPROJ_ROWS = 512
PROJ_ROWS = 256
DA_TQ = 512
MLA_TQ = 1024
ATT_TK = 512
MOE_CHUNK = 256
ROUTE_ROWS = 256
VMEM_LIMIT = 48 * 1024 * 1024
INPROJ_VMEM_LIMIT = 56 * 1024 * 1024

F32 = jnp.float32
BF16 = jnp.bfloat16


def _const_spec(shape):
    nd = len(shape)
    return pl.BlockSpec(shape, lambda *_: (0,) * nd, pipeline_mode=pl.Buffered(1))


def _rms(v, n):
    return lax.rsqrt(jnp.sum(v * v, axis=-1, keepdims=True) * (1.0 / n) + NORM_EPS)


def _dot(a, b):
    return jnp.dot(a, b, preferred_element_type=F32)


def _dot_nt(a, b):
    return lax.dot_general(a, b, (((1,), (1,)), ((), ())), preferred_element_type=F32)


def _in_proj_kernel(x_ref, g_ref, wq_ref, wk_ref, wv_ref, wcq_ref, wckv_ref, wkr_ref, wg_ref,
                    wuq_ref, wukv_ref, bg_ref, qg_ref, kg_ref, qlg_ref, kvlg_ref, mqn_ref, mqr_ref,
                    mkn_ref, mkr_ref, cda_ref, sda_ref, cm_ref, sm_ref,
                    q_ref, k_ref, v_ref, qn_ref, qr_ref, kn_ref, vm_ref, kr_ref, gate_ref):
    x = x_ref[...]
    h = (x * _rms(x, D_MODEL) * g_ref[...]).astype(BF16)
    rows = x.shape[0]
    lane = lax.broadcasted_iota(jnp.int32, (rows, LANES), 1)
    lo = lane < DA_HEAD_DIM
    da_first = (lane & (DA_HEAD_DIM - 1)) < (DA_ROT_DIM // 2)
    m_first = lane < (MLA_ROPE // 2)
    cda, sda = cda_ref[...], sda_ref[...]
    cm, sm = cm_ref[...], sm_ref[...]

    def da_rope(y):
        partner = jnp.where(da_first, pltpu.roll(y, LANES - DA_ROT_DIM // 2, 1),
                            pltpu.roll(y, DA_ROT_DIM // 2, 1))
        return y * cda + partner * sda

    def mla_rope(y):
        partner = jnp.where(m_first, pltpu.roll(y, LANES - MLA_ROPE // 2, 1),
                            pltpu.roll(y, MLA_ROPE // 2, 1))
        return y * cm + partner * sm

    for w_ref, gain_ref, o_ref in ((wq_ref, qg_ref, q_ref), (wk_ref, kg_ref, k_ref)):
        p_all = _dot(h, w_ref[...])
        gain = gain_ref[...]
        for hd in range(DA_HEADS):
            sl = slice(hd * LANES, (hd + 1) * LANES)
            p = p_all[:, sl]
            p2 = p * p
            s_all = jnp.sum(p2, axis=-1, keepdims=True)
            s_lo = jnp.sum(jnp.where(lo, p2, 0.0), axis=-1, keepdims=True)
            inv = jnp.where(lo, lax.rsqrt(s_lo * (1.0 / DA_HEAD_DIM) + NORM_EPS),
                            lax.rsqrt((s_all - s_lo) * (1.0 / DA_HEAD_DIM) + NORM_EPS))
            o_ref[:, sl] = da_rope(p * inv * gain).astype(BF16)
    v_ref[...] = _dot(h, wv_ref[...]).astype(BF16)

    cq = _dot(h, wcq_ref[...])
    cqn = (cq * _rms(cq, MLA_Q_LORA) * qlg_ref[...]).astype(BF16)
    qm = _dot(cqn, wuq_ref[...])
    gn, gr = mqn_ref[...], mqr_ref[...]
    for hd in range(MLA_HEADS):
        sl = slice(hd * LANES, (hd + 1) * LANES)
        qn = qm[:, sl]
        qr = qm[:, D_MODEL + hd * LANES:D_MODEL + (hd + 1) * LANES]
        ss = jnp.sum(qn * qn, axis=-1, keepdims=True) + jnp.sum(qr * qr, axis=-1, keepdims=True)
        inv = lax.rsqrt(ss * (1.0 / (MLA_NOPE + MLA_ROPE)) + NORM_EPS)
        qn_ref[:, sl] = (qn * inv * gn).astype(BF16)
        qr_ref[:, sl] = mla_rope(qr * inv * gr).astype(BF16)

    ckv = _dot(h, wckv_ref[...])
    ckvn = (ckv * _rms(ckv, MLA_KV_LORA) * kvlg_ref[...]).astype(BF16)
    kv = _dot(ckvn, wukv_ref[...])
    gkn = mkn_ref[...]
    for hd in range(MLA_HEADS):
        sl = slice(hd * LANES, (hd + 1) * LANES)
        kn = kv[:, sl]
        kn_ref[:, sl] = (kn * _rms(kn, MLA_NOPE) * gkn).astype(BF16)
    vm_ref[...] = kv[:, D_MODEL:].astype(BF16)
    kr = _dot(h, wkr_ref[...])
    kr_ref[...] = mla_rope(kr * _rms(kr, MLA_ROPE) * mkr_ref[...]).astype(BF16)

    gl = _dot(h, wg_ref[...]) + bg_ref[...]
    gate_ref[...] = (1.0 / (1.0 + jnp.exp(-gl))).astype(BF16)


def _in_proj(x2, g, wts, vecs, tables):
    T = x2.shape[0]
    tm = INPROJ_ROWS
    row = lambda w: pl.BlockSpec((tm, w), lambda i: (i, 0))
    in_specs = ([row(D_MODEL), _const_spec(g.shape)] + [_const_spec(w.shape) for w in wts]
                + [_const_spec(v.shape) for v in vecs] + [row(LANES)] * 4)
    outs = [jax.ShapeDtypeStruct((T, D_MODEL), BF16)] * 7 + [
        jax.ShapeDtypeStruct((T, LANES), BF16), jax.ShapeDtypeStruct((T, 2 * D_MODEL), BF16)]
    out_specs = [row(D_MODEL)] * 7 + [row(LANES), row(2 * D_MODEL)]
    return pl.pallas_call(
        _in_proj_kernel, out_shape=outs, grid=(T // tm,), in_specs=in_specs, out_specs=out_specs,
        compiler_params=pltpu.CompilerParams(dimension_semantics=("parallel",),
                                             vmem_limit_bytes=INPROJ_VMEM_LIMIT),
        name="in_proj")(x2, g, *wts, *vecs, *tables)


def _flash(qs, load_k, load_v, qi, tq, n_maps, sa_ref, sb_ref, m_ref, acc_ref):
    rows = n_maps * tq
    tk = ATT_TK
    assert tk % tq == 0 or tq == 2 * tk
    m_ref[...] = jnp.full(m_ref.shape, -jnp.inf, F32)
    acc_ref[...] = jnp.zeros(acc_ref.shape, F32)

    def scores(j, s_ref):
        s_ref[...] = _dot_nt(qs, load_k(j))

    def update(j, s_ref, masked):
        s = s_ref[...]
        if masked:
            qpos = qi * tq + (lax.broadcasted_iota(jnp.int32, (rows, tk), 0) & (tq - 1))
            kpos = j * tk + lax.broadcasted_iota(jnp.int32, (rows, tk), 1)
            s = jnp.where(kpos <= qpos, s, -jnp.inf)
        m_old = m_ref[...]
        m_new = jnp.maximum(m_old, jnp.max(s, axis=-1, keepdims=True))
        alpha = jnp.exp2(m_old - m_new)
        p = jnp.exp2(s - m_new).astype(BF16)
        v = load_v(j)
        acc_ref[...] = alpha * acc_ref[...] + _dot(p, jnp.concatenate([v, jnp.ones_like(v)], axis=1))
        m_ref[...] = m_new

    n_full = (qi * tq) // tk
    n_pairs = n_full // 2

    scores(0, sa_ref)

    def body(i, carry):
        scores(2 * i + 1, sb_ref)
        update(2 * i, sa_ref, False)
        scores(2 * i + 2, sa_ref)
        update(2 * i + 1, sb_ref, False)
        return carry

    lax.fori_loop(0, n_pairs, body, 0)
    e0 = 2 * n_pairs

    if tq == 2 * tk:
        scores(e0 + 1, sb_ref)
        update(e0, sa_ref, True)
        update(e0 + 1, sb_ref, True)
    else:
        tail2 = n_full - e0

        @pl.when(tail2 == 1)
        def _():
            scores(e0 + 1, sb_ref)
            update(e0, sa_ref, False)
            update(e0 + 1, sb_ref, True)

        @pl.when(tail2 == 0)
        def _():
            update(e0, sa_ref, True)

    return acc_ref[:, :LANES], acc_ref[:, LANES:LANES + 1]


def _flash_scratch(rows):
    return [pltpu.VMEM((rows, ATT_TK), F32), pltpu.VMEM((rows, ATT_TK), F32),
            pltpu.VMEM((rows, 1), F32), pltpu.VMEM((rows, 2 * LANES), F32)]


def _kv_block(j):
    return pl.ds(pl.multiple_of(j * ATT_TK, ATT_TK), ATT_TK)


def _da_attn_kernel(q_ref, k_ref, v_ref, lam_ref, sg_ref, o_ref, *scratch, lambda_init):
    qi = pl.program_id(2)
    tq = DA_TQ
    q = q_ref[...]
    lane = lax.broadcasted_iota(jnp.int32, q.shape, 1)
    zero = jnp.zeros_like(q)
    qs = jnp.concatenate([jnp.where(lane < DA_HEAD_DIM, q, zero),
                          jnp.where(lane >= DA_HEAD_DIM, q, zero)], axis=0)

    acc, l = _flash(qs, lambda j: k_ref[_kv_block(j), :], lambda j: v_ref[_kv_block(j), :],
                    qi, tq, 2, *scratch)
    o1 = acc[:tq] / l[:tq]
    o2 = acc[tq:] / l[tq:]
    lv = lam_ref[...]
    lam = (jnp.exp(jnp.sum(lv[0:1] * lv[1:2], axis=-1, keepdims=True))
           - jnp.exp(jnp.sum(lv[2:3] * lv[3:4], axis=-1, keepdims=True)) + lambda_init)
    o = o1 - lam * o2
    y = o * _rms(o, DA_V_DIM) * sg_ref[...] * (1.0 - lambda_init)
    o_ref[...] = y.astype(BF16)


def _da_attn(q, k, v, lam_vecs, subln_g, lambda_init):
    B, S, _ = q.shape
    tq = DA_TQ
    qspec = pl.BlockSpec((None, tq, LANES), lambda b, h, i: (b, i, h))
    kvspec = pl.BlockSpec((None, S, LANES), lambda b, h, i: (b, 0, h))
    return pl.pallas_call(
        functools.partial(_da_attn_kernel, lambda_init=lambda_init),
        out_shape=jax.ShapeDtypeStruct((B, S, D_MODEL), BF16),
        grid=(B, DA_HEADS, S // tq),
        in_specs=[qspec, kvspec, kvspec, _const_spec(lam_vecs.shape), _const_spec(subln_g.shape)],
        out_specs=qspec,
        scratch_shapes=_flash_scratch(2 * tq),
        compiler_params=pltpu.CompilerParams(
            dimension_semantics=("parallel", "parallel", "arbitrary"), vmem_limit_bytes=VMEM_LIMIT),
        name="da_attn")(q, k, v, lam_vecs, subln_g)


def _mla_attn_kernel(qn_ref, qr_ref, kn_ref, kr_ref, v_ref, o_ref, *scratch):
    qi = pl.program_id(2)
    qs = jnp.concatenate([qn_ref[...], qr_ref[...]], axis=-1)

    def load_k(j):
        return jnp.concatenate([kn_ref[_kv_block(j), :], kr_ref[_kv_block(j), :]], axis=-1)

    acc, l = _flash(qs, load_k, lambda j: v_ref[_kv_block(j), :], qi, MLA_TQ, 1, *scratch)
    o_ref[...] = (acc / l).astype(BF16)


def _mla_attn(qn, qr, kn, kr, vm):
    B, S, _ = qn.shape
    tq = MLA_TQ
    qspec = pl.BlockSpec((None, tq, LANES), lambda b, h, i: (b, i, h))
    kvspec = pl.BlockSpec((None, S, LANES), lambda b, h, i: (b, 0, h))
    krspec = pl.BlockSpec((None, S, LANES), lambda b, h, i: (b, 0, 0))
    return pl.pallas_call(
        _mla_attn_kernel,
        out_shape=jax.ShapeDtypeStruct((B, S, D_MODEL), BF16),
        grid=(B, MLA_HEADS, S // tq),
        in_specs=[qspec, qspec, kvspec, krspec, kvspec],
        out_specs=qspec,
        scratch_shapes=_flash_scratch(tq),
        compiler_params=pltpu.CompilerParams(
            dimension_semantics=("parallel", "parallel", "arbitrary"), vmem_limit_bytes=VMEM_LIMIT),
        name="mla_attn")(qn, qr, kn, kr, vm)


def _merge_kernel(oda_ref, omla_ref, gate_ref, x_ref, wo_ref, fg_ref, whi_ref, wlo_ref, br_ref,
                  x1_ref, h2_ref, ri_ref, rw_ref, cnt_ref, carry_ref):
    i = pl.program_id(0)
    rows = x_ref.shape[0]

    @pl.when(i == 0)
    def _():
        carry_ref[...] = jnp.zeros_like(carry_ref)

    g = gate_ref[...].astype(F32)
    mixed = g[:, :D_MODEL] * oda_ref[...].astype(F32) + g[:, D_MODEL:] * omla_ref[...].astype(F32)
    x1 = x_ref[...] + _dot(mixed.astype(BF16), wo_ref[...])
    x1_ref[...] = x1
    h2 = x1 * _rms(x1, D_MODEL) * fg_ref[...]
    for s in range(ROW_CHUNKS):
        h2_ref[pl.ds(s, rows, stride=ROW_CHUNKS), :] = h2[:, s * LANES:(s + 1) * LANES]

    hi = h2.astype(BF16)
    lo = (h2 - hi.astype(F32)).astype(BF16)
    whi = whi_ref[...]
    logits = _dot(hi, whi) + _dot(hi, wlo_ref[...]) + _dot(lo, whi) + br_ref[...]

    lane = lax.broadcasted_iota(jnp.int32, (rows, LANES), 1)
    neg = -jnp.inf
    gl = jnp.where(lane < N_GROUPS, logits, neg)
    gmax = jnp.max(gl, axis=-1, keepdims=True)
    g_sel = jnp.min(jnp.where(gl == gmax, lane, LANES), axis=-1, keepdims=True)
    p_g = 1.0 / jnp.sum(jnp.exp(gl - gmax), axis=-1, keepdims=True)
    e_lo = N_GROUPS + g_sel * EXPERTS_PER_GROUP
    el = jnp.where((lane >= e_lo) & (lane < e_lo + EXPERTS_PER_GROUP), logits, neg)
    v0 = jnp.max(el, axis=-1, keepdims=True)
    i0 = jnp.min(jnp.where(el == v0, lane, LANES), axis=-1, keepdims=True)
    el2 = jnp.where(lane == i0, neg, el)
    v1 = jnp.max(el2, axis=-1, keepdims=True)
    i1 = jnp.min(jnp.where(el2 == v1, lane, LANES), axis=-1, keepdims=True)
    t = jnp.exp(v1 - v0)
    w0 = p_g / (1.0 + t)
    w1 = p_g * t / (1.0 + t)
    e0 = i0 - N_GROUPS
    e1 = i1 - N_GROUPS

    oh = jnp.where((lane == e0) | (lane == e1), 1.0, 0.0)
    r_i = lax.broadcasted_iota(jnp.int32, (rows, rows), 0)
    c_i = lax.broadcasted_iota(jnp.int32, (rows, rows), 1)
    tri = jnp.where(c_i < r_i, 1.0, 0.0).astype(BF16)
    before = _dot(tri, oh.astype(BF16)) + carry_ref[...]
    rank0 = jnp.sum(jnp.where(lane == e0, before, 0.0), axis=-1, keepdims=True).astype(jnp.int32)
    rank1 = jnp.sum(jnp.where(lane == e1, before, 0.0), axis=-1, keepdims=True).astype(jnp.int32)
    carry = carry_ref[...] + jnp.sum(oh, axis=0, keepdims=True)
    carry_ref[...] = carry
    cnt_ref[...] = carry

    ri = jnp.where(lane == 0, e0, jnp.where(lane == 1, e1, jnp.where(lane == 2, rank0, rank1)))
    ri_ref[...] = ri[:, :SUBLANES]
    rw_ref[...] = jnp.where(lane == 0, w0, w1)[:, :SUBLANES]


def _merge(oda, omla, gates, x2, wo, fg, whi, wlo, br):
    T = x2.shape[0]
    tm = PROJ_ROWS
    row = lambda w: pl.BlockSpec((tm, w), lambda i: (i, 0))
    outs = [jax.ShapeDtypeStruct((T, D_MODEL), F32),
            jax.ShapeDtypeStruct((T * ROW_CHUNKS, LANES), F32),
            jax.ShapeDtypeStruct((T, SUBLANES), jnp.int32),
            jax.ShapeDtypeStruct((T, SUBLANES), F32),
            jax.ShapeDtypeStruct((1, LANES), F32)]
    out_specs = [row(D_MODEL), pl.BlockSpec((tm * ROW_CHUNKS, LANES), lambda i: (i, 0)),
                 row(SUBLANES), row(SUBLANES), pl.BlockSpec((1, LANES), lambda i: (0, 0))]
    return pl.pallas_call(
        _merge_kernel, out_shape=outs, grid=(T // tm,),
        in_specs=[row(D_MODEL), row(D_MODEL), row(2 * D_MODEL), row(D_MODEL), _const_spec(wo.shape),
                  _const_spec(fg.shape), _const_spec(whi.shape), _const_spec(wlo.shape),
                  _const_spec(br.shape)],
        out_specs=out_specs,
        scratch_shapes=[pltpu.VMEM((1, LANES), F32)],
        compiler_params=pltpu.CompilerParams(dimension_semantics=("arbitrary",),
                                             vmem_limit_bytes=VMEM_LIMIT),
        name="merge_router")(oda, omla, gates, x2, wo, fg, whi, wlo, br)


def _row(ref, r):
    return ref.at[pl.ds(pl.multiple_of(r * ROW_CHUNKS, ROW_CHUNKS), ROW_CHUNKS), :]


def _expert_kernel(ce_ref, nu_ref, tok_ref, tok_next_ref, h2_ref, w1_ref, w3_ref, w2_ref, yb_ref,
                   buf_ref, sem):
    c = pl.program_id(0)
    ch = MOE_CHUNK
    slot = c & 1
    n_used = nu_ref[0]

    def row_copy(toks, sl, t):
        return pltpu.make_async_copy(_row(h2_ref, toks[0, 0, t]), _row(buf_ref.at[sl], t),
                                     sem.at[sl])

    def slot_wait(sl):
        pltpu.make_async_copy(h2_ref.at[pl.ds(0, ch * ROW_CHUNKS), :], buf_ref.at[sl],
                              sem.at[sl]).wait()

    @pl.when(c == 0)
    def _():
        def issue(t, carry):
            row_copy(tok_ref, 0, t).start()
            return carry

        lax.fori_loop(0, ch, issue, 0, unroll=8)

    @pl.when(c < n_used)
    def _():
        slot_wait(slot)
        for t in range(ch):
            row_copy(tok_next_ref, 1 - slot, t).start()
        x = jnp.concatenate([buf_ref[slot, pl.ds(s, ch, stride=ROW_CHUNKS), :]
                             for s in range(ROW_CHUNKS)], axis=-1).astype(BF16)
        a = _dot(x, w1_ref[...])
        b = _dot(x, w3_ref[...])
        hmid = (a / (1.0 + jnp.exp(-a))) * b
        y = _dot(hmid.astype(BF16), w2_ref[...])
        for s in range(ROW_CHUNKS):
            yb_ref[pl.ds(s, ch, stride=ROW_CHUNKS), :] = y[:, s * LANES:(s + 1) * LANES]

    @pl.when(c >= n_used)
    def _():
        yb_ref[...] = jnp.zeros_like(yb_ref)

    @pl.when(c == n_used)
    def _():
        slot_wait(slot)


def _experts(chunk_e, n_used, tok3, h2r, w1, w3, w2):
    ch = MOE_CHUNK
    n_chunks = tok3.shape[0]
    smem = lambda f: pl.BlockSpec((1, 1, ch), f, memory_space=pltpu.SMEM)
    wspec = lambda s: pl.BlockSpec((None,) + s, lambda c, ce, nu: (ce[c], 0, 0))
    return pl.pallas_call(
        _expert_kernel,
        out_shape=jax.ShapeDtypeStruct((n_chunks * ch * ROW_CHUNKS, LANES), F32),
        grid_spec=pltpu.PrefetchScalarGridSpec(
            num_scalar_prefetch=2, grid=(n_chunks,),
            in_specs=[smem(lambda c, ce, nu: (c, 0, 0)),
                      smem(lambda c, ce, nu: (jnp.minimum(c + 1, n_chunks - 1), 0, 0)),
                      pl.BlockSpec(memory_space=pl.ANY),
                      wspec((D_MODEL, D_EXPERT)), wspec((D_MODEL, D_EXPERT)),
                      wspec((D_EXPERT, D_MODEL))],
            out_specs=pl.BlockSpec((ch * ROW_CHUNKS, LANES), lambda c, ce, nu: (c, 0)),
            scratch_shapes=[pltpu.VMEM((2, ch * ROW_CHUNKS, LANES), F32),
                            pltpu.SemaphoreType.DMA((2,))]),
        compiler_params=pltpu.CompilerParams(dimension_semantics=("arbitrary",),
                                             vmem_limit_bytes=VMEM_LIMIT),
        name="experts")(chunk_e, n_used, tok3, tok3, h2r, w1, w3, w2)


def _combine_kernel(dest_ref, dest_next_ref, yb_ref, x1_ref, rw_ref, o_ref, buf_ref, sem):
    i = pl.program_id(0)
    n = pl.num_programs(0)
    tb = ROUTE_ROWS
    slot = i & 1

    def gather(dests, sl):
        def issue(t, c):
            for k in range(TOP_K):
                pltpu.make_async_copy(_row(yb_ref, dests[0, 0, TOP_K * t + k]),
                                      _row(buf_ref.at[sl, k], t), sem.at[sl]).start()
            return c

        lax.fori_loop(0, tb, issue, 0, unroll=4)

    @pl.when(i == 0)
    def _():
        gather(dest_ref, 0)

    @pl.when(i + 1 < n)
    def _():
        gather(dest_next_ref, 1 - slot)

    for k in range(TOP_K):
        pltpu.make_async_copy(yb_ref.at[pl.ds(0, tb * ROW_CHUNKS), :], buf_ref.at[slot, k],
                              sem.at[slot]).wait()
    rw = rw_ref[...]
    acc = x1_ref[...]
    for k in range(TOP_K):
        y = jnp.concatenate([buf_ref[slot, k, pl.ds(s, tb, stride=ROW_CHUNKS), :]
                             for s in range(ROW_CHUNKS)], axis=-1)
        acc = acc + y * rw[:, k:k + 1]
    o_ref[...] = acc


def _combine(dest3, yb, x1, rw):
    T = x1.shape[0]
    tb = ROUTE_ROWS
    nblk = T // tb
    smem = lambda f: pl.BlockSpec((1, 1, TOP_K * tb), f, memory_space=pltpu.SMEM)
    return pl.pallas_call(
        _combine_kernel,
        out_shape=jax.ShapeDtypeStruct((T, D_MODEL), F32),
        grid=(nblk,),
        in_specs=[smem(lambda i: (i, 0, 0)), smem(lambda i: (jnp.minimum(i + 1, nblk - 1), 0, 0)),
                  pl.BlockSpec(memory_space=pl.ANY),
                  pl.BlockSpec((tb, D_MODEL), lambda i: (i, 0)),
                  pl.BlockSpec((tb, SUBLANES), lambda i: (i, 0))],
        out_specs=pl.BlockSpec((tb, D_MODEL), lambda i: (i, 0)),
        scratch_shapes=[pltpu.VMEM((2, TOP_K, tb * ROW_CHUNKS, LANES), F32),
                        pltpu.SemaphoreType.DMA((2,))],
        compiler_params=pltpu.CompilerParams(dimension_semantics=("arbitrary",),
                                             vmem_limit_bytes=VMEM_LIMIT),
        name="combine")(dest3, dest3, yb, x1, rw)


def _rope_tables(positions, dim, half_lanes, one_fill):
    inv = ROPE_THETA ** (-jnp.arange(0, dim, 2, dtype=F32) / dim)
    ang = positions.astype(F32).reshape(-1, 1) * inv
    c, s = jnp.cos(ang), jnp.sin(ang)
    T = ang.shape[0]
    fill = half_lanes - dim
    cseg = jnp.concatenate([c, c, jnp.full((T, fill), one_fill, F32)], axis=-1)
    sseg = jnp.concatenate([-s, s, jnp.zeros((T, fill), F32)], axis=-1)
    reps = LANES // half_lanes
    return jnp.tile(cseg, (1, reps)), jnp.tile(sseg, (1, reps))


def _layer(x2, B, S, tables, lambda_init, attn_norm_g, w_in, b_gate, da_q_norm_g, da_k_norm_g,
           da_lambda_q1, da_lambda_k1, da_lambda_q2, da_lambda_k2, da_subln_g, mla_q_lora_g,
           mla_w_uq, mla_kv_lora_g, mla_w_ukv, mla_q_norm_g, mla_k_nope_norm_g, mla_k_rope_norm_g,
           w_o, ffn_norm_g, w_group, b_group, w_router, b_router, w1, w3, w2):
    T = B * S
    D = D_MODEL
    c0, c1, c2 = D, 2 * D, 3 * D
    c3 = c2 + MLA_Q_LORA
    c4 = c3 + MLA_KV_LORA
    c5 = c4 + MLA_ROPE
    wb = w_in.astype(BF16)
    wkr = jnp.pad(wb[:, c4:c5], ((0, 0), (0, LANES - MLA_ROPE)))
    qk = MLA_NOPE + MLA_ROPE
    uq = mla_w_uq.astype(BF16).reshape(MLA_Q_LORA, MLA_HEADS, qk)
    wuq = jnp.concatenate([
        uq[:, :, :MLA_NOPE].reshape(MLA_Q_LORA, D),
        jnp.pad(uq[:, :, MLA_NOPE:], ((0, 0), (0, 0), (0, LANES - MLA_ROPE))).reshape(MLA_Q_LORA, D)],
        axis=1)
    ukv = mla_w_ukv.astype(BF16).reshape(MLA_KV_LORA, MLA_HEADS, MLA_NOPE + MLA_V)
    wukv = jnp.concatenate([ukv[:, :, :MLA_NOPE].reshape(MLA_KV_LORA, D),
                            ukv[:, :, MLA_NOPE:].reshape(MLA_KV_LORA, D)], axis=1)
    wts = (wb[:, :c0], wb[:, c0:c1], wb[:, c1:c2], wb[:, c2:c3], wb[:, c3:c4], wkr, wb[:, c5:],
           wuq, wukv)
    r1 = lambda v: v.astype(F32).reshape(1, -1)
    pad_rope = lambda v: jnp.pad(v.astype(F32), (0, LANES - MLA_ROPE)).reshape(1, LANES)
    da_scale = DA_HEAD_DIM ** -0.5 * LOG2_E
    mla_scale = qk ** -0.5 * LOG2_E
    vecs = (r1(b_gate),
            jnp.tile(r1(da_q_norm_g), (1, 2)) * da_scale,
            jnp.tile(r1(da_k_norm_g), (1, 2)),
            r1(mla_q_lora_g), r1(mla_kv_lora_g),
            r1(mla_q_norm_g[:MLA_NOPE]) * mla_scale,
            pad_rope(mla_q_norm_g[MLA_NOPE:]) * mla_scale,
            r1(mla_k_nope_norm_g), pad_rope(mla_k_rope_norm_g))
    q, k, v, qn, qr, kn, vm, kr, gates = _in_proj(x2, r1(attn_norm_g), wts, vecs, tables)

    b3 = lambda a: a.reshape(B, S, a.shape[-1])
    lam_vecs = jnp.stack([da_lambda_q1, da_lambda_k1, da_lambda_q2, da_lambda_k2]).astype(F32)
    o_da = _da_attn(b3(q), b3(k), b3(v), lam_vecs, r1(da_subln_g), lambda_init)
    o_mla = _mla_attn(b3(qn), b3(qr), b3(kn), b3(kr), b3(vm))

    wgr = jnp.concatenate([w_group.astype(F32), w_router.astype(F32),
                           jnp.zeros((D, LANES - N_GROUPS - N_EXPERTS), F32)], axis=1)
    whi = wgr.astype(BF16)
    wlo = (wgr - whi.astype(F32)).astype(BF16)
    br = jnp.concatenate([b_group.astype(F32), b_router.astype(F32),
                          jnp.zeros((LANES - N_GROUPS - N_EXPERTS,), F32)]).reshape(1, LANES)
    x1, h2r, ri, rw, cnt = _merge(o_da.reshape(T, D), o_mla.reshape(T, D), gates, x2,
                                  w_o.astype(BF16), r1(ffn_norm_g), whi, wlo, br)

    ch = MOE_CHUNK
    counts = cnt[0, :N_EXPERTS].astype(jnp.int32)
    padded = (counts + ch - 1) // ch * ch
    pends = jnp.cumsum(padded)
    pstarts = pends - padded
    n_chunks = (T * TOP_K) // ch + N_EXPERTS
    P = n_chunks * ch
    e_ids = jnp.arange(N_EXPERTS, dtype=jnp.int32)
    seg_start = jnp.sum(jnp.where(ri[:, :TOP_K, None] == e_ids, pstarts, 0), axis=-1)
    dest = seg_start + ri[:, TOP_K:2 * TOP_K]
    dest3 = dest.reshape(T // ROUTE_ROWS, 1, TOP_K * ROUTE_ROWS)
    chunk_start = jnp.arange(n_chunks, dtype=jnp.int32)[:, None] * ch
    chunk_e = jnp.minimum(jnp.sum((pends[None, :] <= chunk_start).astype(jnp.int32), axis=1),
                          N_EXPERTS - 1).astype(jnp.int32)
    n_used = (pends[-1:] // ch).astype(jnp.int32)
    tok = jnp.repeat(jnp.arange(T, dtype=jnp.int32), TOP_K)
    tok3 = jnp.zeros((P,), jnp.int32).at[dest.reshape(-1)].set(
        tok, unique_indices=True, mode='promise_in_bounds').reshape(n_chunks, 1, ch)

    yb = _experts(chunk_e, n_used, tok3, h2r, w1.astype(BF16), w3.astype(BF16), w2.astype(BF16))
    return _combine(dest3, yb, x1, rw)


def kernel(x, positions, attn_norm_g, w_in, b_gate, da_q_norm_g, da_k_norm_g, da_lambda_q1, da_lambda_k1, da_lambda_q2, da_lambda_k2, da_subln_g, mla_q_lora_g, mla_w_uq, mla_kv_lora_g, mla_w_ukv, mla_q_norm_g, mla_k_nope_norm_g, mla_k_rope_norm_g, w_o, ffn_norm_g, w_group, b_group, w_router, b_router, w1, w3, w2):
    B, S, D = x.shape
    assert D == D_MODEL and S % max(DA_TQ, MLA_TQ, ATT_TK) == 0 and (B * S) % INPROJ_ROWS == 0
    tables = (_rope_tables(positions, DA_ROT_DIM, DA_HEAD_DIM, 1.0)
              + _rope_tables(positions, MLA_ROPE, LANES, 1.0))
    x2 = x.reshape(B * S, D)
    per_layer = (attn_norm_g, w_in, b_gate, da_q_norm_g, da_k_norm_g, da_lambda_q1, da_lambda_k1,
                 da_lambda_q2, da_lambda_k2, da_subln_g, mla_q_lora_g, mla_w_uq, mla_kv_lora_g,
                 mla_w_ukv, mla_q_norm_g, mla_k_nope_norm_g, mla_k_rope_norm_g, w_o, ffn_norm_g,
                 w_group, b_group, w_router, b_router, w1, w3, w2)
    for l in range(w_in.shape[0]):
        lambda_init = 0.8 - 0.6 * math.exp(-0.3 * l)
        x2 = _layer(x2, B, S, tables, lambda_init, *[p[l] for p in per_layer])
    return x2.reshape(B, S, D)
```

```python
import functools
import math

import jax
import jax.numpy as jnp
from jax import lax
from jax.experimental import pallas as pl
from jax.experimental.pallas import tpu as pltpu

D_MODEL = 1024
DA_HEADS = 8
DA_HEAD_DIM = 64
DA_V_DIM = 128
DA_ROT_DIM = 16
MLA_HEADS = 8
MLA_Q_LORA = 384
MLA_KV_LORA = 256
MLA_NOPE = 128
MLA_ROPE = 64
MLA_V = 128
ROPE_THETA = 500000.0
N_GROUPS = 4
EXPERTS_PER_GROUP = 8
N_EXPERTS = N_GROUPS * EXPERTS_PER_GROUP
TOP_K = 2
D_EXPERT = 512
NORM_EPS = 1e-6
LOG2_E = math.log2(math.e)

LANES = 128
SUBLANES = 8
ROW_CHUNKS = D_MODEL // LANES

INPROJ_ROWS = 512
PROJ_ROWS = 512
DA_TQ = 512
MLA_TQ = 1024
ATT_TK = 512
MOE_CHUNK = 512
ROUTE_ROWS = 256
VMEM_LIMIT = 48 * 1024 * 1024
INPROJ_VMEM_LIMIT = 56 * 1024 * 1024

F32 = jnp.float32
BF16 = jnp.bfloat16


def _const_spec(shape):
    nd = len(shape)
    return pl.BlockSpec(shape, lambda *_: (0,) * nd, pipeline_mode=pl.Buffered(1))


def _rms(v, n):
    return lax.rsqrt(jnp.sum(v * v, axis=-1, keepdims=True) * (1.0 / n) + NORM_EPS)


def _dot(a, b):
    return jnp.dot(a, b, preferred_element_type=F32)


def _dot_nt(a, b):
    return lax.dot_general(a, b, (((1,), (1,)), ((), ())), preferred_element_type=F32)


def _in_proj_kernel(x_ref, g_ref, wq_ref, wk_ref, wv_ref, wcq_ref, wckv_ref, wkr_ref, wg_ref,
                    wuq_ref, wukv_ref, bg_ref, qg_ref, kg_ref, qlg_ref, kvlg_ref, mqn_ref, mqr_ref,
                    mkn_ref, mkr_ref, cda_ref, sda_ref, cm_ref, sm_ref,
                    q_ref, k_ref, v_ref, qn_ref, qr_ref, kn_ref, vm_ref, kr_ref, gate_ref):
    x = x_ref[...]
    h = (x * _rms(x, D_MODEL) * g_ref[...]).astype(BF16)
    rows = x.shape[0]
    lane = lax.broadcasted_iota(jnp.int32, (rows, LANES), 1)
    lo = lane < DA_HEAD_DIM
    da_first = (lane & (DA_HEAD_DIM - 1)) < (DA_ROT_DIM // 2)
    m_first = lane < (MLA_ROPE // 2)
    cda, sda = cda_ref[...], sda_ref[...]
    cm, sm = cm_ref[...], sm_ref[...]

    def da_rope(y):
        partner = jnp.where(da_first, pltpu.roll(y, LANES - DA_ROT_DIM // 2, 1),
                            pltpu.roll(y, DA_ROT_DIM // 2, 1))
        return y * cda + partner * sda

    def mla_rope(y):
        partner = jnp.where(m_first, pltpu.roll(y, LANES - MLA_ROPE // 2, 1),
                            pltpu.roll(y, MLA_ROPE // 2, 1))
        return y * cm + partner * sm

    for w_ref, gain_ref, o_ref in ((wq_ref, qg_ref, q_ref), (wk_ref, kg_ref, k_ref)):
        p_all = _dot(h, w_ref[...])
        gain = gain_ref[...]
        for hd in range(DA_HEADS):
            sl = slice(hd * LANES, (hd + 1) * LANES)
            p = p_all[:, sl]
            p2 = p * p
            s_all = jnp.sum(p2, axis=-1, keepdims=True)
            s_lo = jnp.sum(jnp.where(lo, p2, 0.0), axis=-1, keepdims=True)
            inv = jnp.where(lo, lax.rsqrt(s_lo * (1.0 / DA_HEAD_DIM) + NORM_EPS),
                            lax.rsqrt((s_all - s_lo) * (1.0 / DA_HEAD_DIM) + NORM_EPS))
            o_ref[:, sl] = da_rope(p * inv * gain).astype(BF16)
    v_ref[...] = _dot(h, wv_ref[...]).astype(BF16)

    cq = _dot(h, wcq_ref[...])
    cqn = (cq * _rms(cq, MLA_Q_LORA) * qlg_ref[...]).astype(BF16)
    qm = _dot(cqn, wuq_ref[...])
    gn, gr = mqn_ref[...], mqr_ref[...]
    for hd in range(MLA_HEADS):
        sl = slice(hd * LANES, (hd + 1) * LANES)
        qn = qm[:, sl]
        qr = qm[:, D_MODEL + hd * LANES:D_MODEL + (hd + 1) * LANES]
        ss = jnp.sum(qn * qn, axis=-1, keepdims=True) + jnp.sum(qr * qr, axis=-1, keepdims=True)
        inv = lax.rsqrt(ss * (1.0 / (MLA_NOPE + MLA_ROPE)) + NORM_EPS)
        qn_ref[:, sl] = (qn * inv * gn).astype(BF16)
        qr_ref[:, sl] = mla_rope(qr * inv * gr).astype(BF16)

    ckv = _dot(h, wckv_ref[...])
    ckvn = (ckv * _rms(ckv, MLA_KV_LORA) * kvlg_ref[...]).astype(BF16)
    kv = _dot(ckvn, wukv_ref[...])
    gkn = mkn_ref[...]
    for hd in range(MLA_HEADS):
        sl = slice(hd * LANES, (hd + 1) * LANES)
        kn = kv[:, sl]
        kn_ref[:, sl] = (kn * _rms(kn, MLA_NOPE) * gkn).astype(BF16)
    vm_ref[...] = kv[:, D_MODEL:].astype(BF16)
    kr = _dot(h, wkr_ref[...])
    kr_ref[...] = mla_rope(kr * _rms(kr, MLA_ROPE) * mkr_ref[...]).astype(BF16)

    gl = _dot(h, wg_ref[...]) + bg_ref[...]
    gate_ref[...] = (1.0 / (1.0 + jnp.exp(-gl))).astype(BF16)


def _in_proj(x2, g, wts, vecs, tables):
    T = x2.shape[0]
    tm = INPROJ_ROWS
    row = lambda w: pl.BlockSpec((tm, w), lambda i: (i, 0))
    in_specs = ([row(D_MODEL), _const_spec(g.shape)] + [_const_spec(w.shape) for w in wts]
                + [_const_spec(v.shape) for v in vecs] + [row(LANES)] * 4)
    outs = [jax.ShapeDtypeStruct((T, D_MODEL), BF16)] * 7 + [
        jax.ShapeDtypeStruct((T, LANES), BF16), jax.ShapeDtypeStruct((T, 2 * D_MODEL), BF16)]
    out_specs = [row(D_MODEL)] * 7 + [row(LANES), row(2 * D_MODEL)]
    return pl.pallas_call(
        _in_proj_kernel, out_shape=outs, grid=(T // tm,), in_specs=in_specs, out_specs=out_specs,
        compiler_params=pltpu.CompilerParams(dimension_semantics=("parallel",),
                                             vmem_limit_bytes=INPROJ_VMEM_LIMIT),
        name="in_proj")(x2, g, *wts, *vecs, *tables)


def _flash(qs, load_k, load_v, qi, tq, n_maps, sa_ref, sb_ref, m_ref, acc_ref):
    rows = n_maps * tq
    tk = ATT_TK
    assert tk % tq == 0 or tq == 2 * tk
    m_ref[...] = jnp.full(m_ref.shape, -jnp.inf, F32)
    acc_ref[...] = jnp.zeros(acc_ref.shape, F32)

    def scores(j, s_ref):
        s_ref[...] = _dot_nt(qs, load_k(j))

    def update(j, s_ref, masked):
        s = s_ref[...]
        if masked:
            qpos = qi * tq + (lax.broadcasted_iota(jnp.int32, (rows, tk), 0) & (tq - 1))
            kpos = j * tk + lax.broadcasted_iota(jnp.int32, (rows, tk), 1)
            s = jnp.where(kpos <= qpos, s, -jnp.inf)
        m_old = m_ref[...]
        m_new = jnp.maximum(m_old, jnp.max(s, axis=-1, keepdims=True))
        alpha = jnp.exp2(m_old - m_new)
        p = jnp.exp2(s - m_new).astype(BF16)
        v = load_v(j)
        acc_ref[...] = alpha * acc_ref[...] + _dot(p, jnp.concatenate([v, jnp.ones_like(v)], axis=1))
        m_ref[...] = m_new

    n_full = (qi * tq) // tk
    n_pairs = n_full // 2

    scores(0, sa_ref)

    def body(i, carry):
        scores(2 * i + 1, sb_ref)
        update(2 * i, sa_ref, False)
        scores(2 * i + 2, sa_ref)
        update(2 * i + 1, sb_ref, False)
        return carry

    lax.fori_loop(0, n_pairs, body, 0)
    e0 = 2 * n_pairs

    if tq == 2 * tk:
        scores(e0 + 1, sb_ref)
        update(e0, sa_ref, True)
        update(e0 + 1, sb_ref, True)
    else:
        tail2 = n_full - e0

        @pl.when(tail2 == 1)
        def _():
            scores(e0 + 1, sb_ref)
            update(e0, sa_ref, False)
            update(e0 + 1, sb_ref, True)

        @pl.when(tail2 == 0)
        def _():
            update(e0, sa_ref, True)

    return acc_ref[:, :LANES], acc_ref[:, LANES:LANES + 1]


def _flash_scratch(rows):
    return [pltpu.VMEM((rows, ATT_TK), F32), pltpu.VMEM((rows, ATT_TK), F32),
            pltpu.VMEM((rows, 1), F32), pltpu.VMEM((rows, 2 * LANES), F32)]


def _kv_block(j):
    return pl.ds(pl.multiple_of(j * ATT_TK, ATT_TK), ATT_TK)


def _da_attn_kernel(q_ref, k_ref, v_ref, lam_ref, sg_ref, o_ref, *scratch, lambda_init):
    qi = pl.program_id(2)
    tq = DA_TQ
    q = q_ref[...]
    lane = lax.broadcasted_iota(jnp.int32, q.shape, 1)
    zero = jnp.zeros_like(q)
    qs = jnp.concatenate([jnp.where(lane < DA_HEAD_DIM, q, zero),
                          jnp.where(lane >= DA_HEAD_DIM, q, zero)], axis=0)

    acc, l = _flash(qs, lambda j: k_ref[_kv_block(j), :], lambda j: v_ref[_kv_block(j), :],
                    qi, tq, 2, *scratch)
    o1 = acc[:tq] / l[:tq]
    o2 = acc[tq:] / l[tq:]
    lv = lam_ref[...]
    lam = (jnp.exp(jnp.sum(lv[0:1] * lv[1:2], axis=-1, keepdims=True))
           - jnp.exp(jnp.sum(lv[2:3] * lv[3:4], axis=-1, keepdims=True)) + lambda_init)
    o = o1 - lam * o2
    y = o * _rms(o, DA_V_DIM) * sg_ref[...] * (1.0 - lambda_init)
    o_ref[...] = y.astype(BF16)


def _da_attn(q, k, v, lam_vecs, subln_g, lambda_init):
    B, S, _ = q.shape
    tq = DA_TQ
    qspec = pl.BlockSpec((None, tq, LANES), lambda b, h, i: (b, i, h))
    kvspec = pl.BlockSpec((None, S, LANES), lambda b, h, i: (b, 0, h))
    return pl.pallas_call(
        functools.partial(_da_attn_kernel, lambda_init=lambda_init),
        out_shape=jax.ShapeDtypeStruct((B, S, D_MODEL), BF16),
        grid=(B, DA_HEADS, S // tq),
        in_specs=[qspec, kvspec, kvspec, _const_spec(lam_vecs.shape), _const_spec(subln_g.shape)],
        out_specs=qspec,
        scratch_shapes=_flash_scratch(2 * tq),
        compiler_params=pltpu.CompilerParams(
            dimension_semantics=("parallel", "parallel", "arbitrary"), vmem_limit_bytes=VMEM_LIMIT),
        name="da_attn")(q, k, v, lam_vecs, subln_g)


def _mla_attn_kernel(qn_ref, qr_ref, kn_ref, kr_ref, v_ref, o_ref, *scratch):
    qi = pl.program_id(2)
    qs = jnp.concatenate([qn_ref[...], qr_ref[...]], axis=-1)

    def load_k(j):
        return jnp.concatenate([kn_ref[_kv_block(j), :], kr_ref[_kv_block(j), :]], axis=-1)

    acc, l = _flash(qs, load_k, lambda j: v_ref[_kv_block(j), :], qi, MLA_TQ, 1, *scratch)
    o_ref[...] = (acc / l).astype(BF16)


def _mla_attn(qn, qr, kn, kr, vm):
    B, S, _ = qn.shape
    tq = MLA_TQ
    qspec = pl.BlockSpec((None, tq, LANES), lambda b, h, i: (b, i, h))
    kvspec = pl.BlockSpec((None, S, LANES), lambda b, h, i: (b, 0, h))
    krspec = pl.BlockSpec((None, S, LANES), lambda b, h, i: (b, 0, 0))
    return pl.pallas_call(
        _mla_attn_kernel,
        out_shape=jax.ShapeDtypeStruct((B, S, D_MODEL), BF16),
        grid=(B, MLA_HEADS, S // tq),
        in_specs=[qspec, qspec, kvspec, krspec, kvspec],
        out_specs=qspec,
        scratch_shapes=_flash_scratch(tq),
        compiler_params=pltpu.CompilerParams(
            dimension_semantics=("parallel", "parallel", "arbitrary"), vmem_limit_bytes=VMEM_LIMIT),
        name="mla_attn")(qn, qr, kn, kr, vm)


def _merge_kernel(oda_ref, omla_ref, gate_ref, x_ref, wo_ref, fg_ref, whi_ref, wlo_ref, br_ref,
                  x1_ref, h2_ref, ri_ref, rw_ref, cnt_ref, carry_ref):
    i = pl.program_id(0)
    rows = x_ref.shape[0]

    @pl.when(i == 0)
    def _():
        carry_ref[...] = jnp.zeros_like(carry_ref)

    g = gate_ref[...].astype(F32)
    mixed = g[:, :D_MODEL] * oda_ref[...].astype(F32) + g[:, D_MODEL:] * omla_ref[...].astype(F32)
    x1 = x_ref[...] + _dot(mixed.astype(BF16), wo_ref[...])
    x1_ref[...] = x1
    h2 = x1 * _rms(x1, D_MODEL) * fg_ref[...]
    for s in range(ROW_CHUNKS):
        h2_ref[pl.ds(s, rows, stride=ROW_CHUNKS), :] = h2[:, s * LANES:(s + 1) * LANES]

    hi = h2.astype(BF16)
    lo = (h2 - hi.astype(F32)).astype(BF16)
    whi = whi_ref[...]
    logits = _dot(hi, whi) + _dot(hi, wlo_ref[...]) + _dot(lo, whi) + br_ref[...]

    lane = lax.broadcasted_iota(jnp.int32, (rows, LANES), 1)
    neg = -jnp.inf
    gl = jnp.where(lane < N_GROUPS, logits, neg)
    gmax = jnp.max(gl, axis=-1, keepdims=True)
    g_sel = jnp.min(jnp.where(gl == gmax, lane, LANES), axis=-1, keepdims=True)
    p_g = 1.0 / jnp.sum(jnp.exp(gl - gmax), axis=-1, keepdims=True)
    e_lo = N_GROUPS + g_sel * EXPERTS_PER_GROUP
    el = jnp.where((lane >= e_lo) & (lane < e_lo + EXPERTS_PER_GROUP), logits, neg)
    v0 = jnp.max(el, axis=-1, keepdims=True)
    i0 = jnp.min(jnp.where(el == v0, lane, LANES), axis=-1, keepdims=True)
    el2 = jnp.where(lane == i0, neg, el)
    v1 = jnp.max(el2, axis=-1, keepdims=True)
    i1 = jnp.min(jnp.where(el2 == v1, lane, LANES), axis=-1, keepdims=True)
    t = jnp.exp(v1 - v0)
    w0 = p_g / (1.0 + t)
    w1 = p_g * t / (1.0 + t)
    e0 = i0 - N_GROUPS
    e1 = i1 - N_GROUPS

    oh = jnp.where((lane == e0) | (lane == e1), 1.0, 0.0)
    r_i = lax.broadcasted_iota(jnp.int32, (rows, rows), 0)
    c_i = lax.broadcasted_iota(jnp.int32, (rows, rows), 1)
    tri = jnp.where(c_i < r_i, 1.0, 0.0).astype(BF16)
    before = _dot(tri, oh.astype(BF16)) + carry_ref[...]
    rank0 = jnp.sum(jnp.where(lane == e0, before, 0.0), axis=-1, keepdims=True).astype(jnp.int32)
    rank1 = jnp.sum(jnp.where(lane == e1, before, 0.0), axis=-1, keepdims=True).astype(jnp.int32)
    carry = carry_ref[...] + jnp.sum(oh, axis=0, keepdims=True)
    carry_ref[...] = carry
    cnt_ref[...] = carry

    ri = jnp.where(lane == 0, e0, jnp.where(lane == 1, e1, jnp.where(lane == 2, rank0, rank1)))
    ri_ref[...] = ri[:, :SUBLANES]
    rw_ref[...] = jnp.where(lane == 0, w0, w1)[:, :SUBLANES]


def _merge(oda, omla, gates, x2, wo, fg, whi, wlo, br):
    T = x2.shape[0]
    tm = PROJ_ROWS
    row = lambda w: pl.BlockSpec((tm, w), lambda i: (i, 0))
    outs = [jax.ShapeDtypeStruct((T, D_MODEL), F32),
            jax.ShapeDtypeStruct((T * ROW_CHUNKS, LANES), F32),
            jax.ShapeDtypeStruct((T, SUBLANES), jnp.int32),
            jax.ShapeDtypeStruct((T, SUBLANES), F32),
            jax.ShapeDtypeStruct((1, LANES), F32)]
    out_specs = [row(D_MODEL), pl.BlockSpec((tm * ROW_CHUNKS, LANES), lambda i: (i, 0)),
                 row(SUBLANES), row(SUBLANES), pl.BlockSpec((1, LANES), lambda i: (0, 0))]
    return pl.pallas_call(
        _merge_kernel, out_shape=outs, grid=(T // tm,),
        in_specs=[row(D_MODEL), row(D_MODEL), row(2 * D_MODEL), row(D_MODEL), _const_spec(wo.shape),
                  _const_spec(fg.shape), _const_spec(whi.shape), _const_spec(wlo.shape),
                  _const_spec(br.shape)],
        out_specs=out_specs,
        scratch_shapes=[pltpu.VMEM((1, LANES), F32)],
        compiler_params=pltpu.CompilerParams(dimension_semantics=("arbitrary",),
                                             vmem_limit_bytes=VMEM_LIMIT),
        name="merge_router")(oda, omla, gates, x2, wo, fg, whi, wlo, br)


def _row(ref, r):
    return ref.at[pl.ds(pl.multiple_of(r * ROW_CHUNKS, ROW_CHUNKS), ROW_CHUNKS), :]


def _expert_kernel(ce_ref, nu_ref, tok_ref, tok_next_ref, h2_ref, w1_ref, w3_ref, w2_ref, yb_ref,
                   buf_ref, sem):
    c = pl.program_id(0)
    ch = MOE_CHUNK
    slot = c & 1
    n_used = nu_ref[0]

    def row_copy(toks, sl, t):
        return pltpu.make_async_copy(_row(h2_ref, toks[0, 0, t]), _row(buf_ref.at[sl], t),
                                     sem.at[sl])

    def slot_wait(sl):
        pltpu.make_async_copy(h2_ref.at[pl.ds(0, ch * ROW_CHUNKS), :], buf_ref.at[sl],
                              sem.at[sl]).wait()

    @pl.when(c == 0)
    def _():
        def issue(t, carry):
            row_copy(tok_ref, 0, t).start()
            return carry

        lax.fori_loop(0, ch, issue, 0, unroll=8)

    @pl.when(c < n_used)
    def _():
        slot_wait(slot)
        for t in range(ch):
            row_copy(tok_next_ref, 1 - slot, t).start()
        x = jnp.concatenate([buf_ref[slot, pl.ds(s, ch, stride=ROW_CHUNKS), :]
                             for s in range(ROW_CHUNKS)], axis=-1).astype(BF16)
        a = _dot(x, w1_ref[...])
        b = _dot(x, w3_ref[...])
        hmid = (a / (1.0 + jnp.exp(-a))) * b
        y = _dot(hmid.astype(BF16), w2_ref[...])
        for s in range(ROW_CHUNKS):
            yb_ref[pl.ds(s, ch, stride=ROW_CHUNKS), :] = y[:, s * LANES:(s + 1) * LANES]

    @pl.when(c >= n_used)
    def _():
        yb_ref[...] = jnp.zeros_like(yb_ref)

    @pl.when(c == n_used)
    def _():
        slot_wait(slot)


def _experts(chunk_e, n_used, tok3, h2r, w1, w3, w2):
    ch = MOE_CHUNK
    n_chunks = tok3.shape[0]
    smem = lambda f: pl.BlockSpec((1, 1, ch), f, memory_space=pltpu.SMEM)
    wspec = lambda s: pl.BlockSpec((None,) + s, lambda c, ce, nu: (ce[c], 0, 0))
    return pl.pallas_call(
        _expert_kernel,
        out_shape=jax.ShapeDtypeStruct((n_chunks * ch * ROW_CHUNKS, LANES), F32),
        grid_spec=pltpu.PrefetchScalarGridSpec(
            num_scalar_prefetch=2, grid=(n_chunks,),
            in_specs=[smem(lambda c, ce, nu: (c, 0, 0)),
                      smem(lambda c, ce, nu: (jnp.minimum(c + 1, n_chunks - 1), 0, 0)),
                      pl.BlockSpec(memory_space=pl.ANY),
                      wspec((D_MODEL, D_EXPERT)), wspec((D_MODEL, D_EXPERT)),
                      wspec((D_EXPERT, D_MODEL))],
            out_specs=pl.BlockSpec((ch * ROW_CHUNKS, LANES), lambda c, ce, nu: (c, 0)),
            scratch_shapes=[pltpu.VMEM((2, ch * ROW_CHUNKS, LANES), F32),
                            pltpu.SemaphoreType.DMA((2,))]),
        compiler_params=pltpu.CompilerParams(dimension_semantics=("arbitrary",),
                                             vmem_limit_bytes=VMEM_LIMIT),
        name="experts")(chunk_e, n_used, tok3, tok3, h2r, w1, w3, w2)


def _combine_kernel(dest_ref, dest_next_ref, yb_ref, x1_ref, rw_ref, o_ref, buf_ref, sem):
    i = pl.program_id(0)
    n = pl.num_programs(0)
    tb = ROUTE_ROWS
    slot = i & 1

    def gather(dests, sl):
        def issue(t, c):
            for k in range(TOP_K):
                pltpu.make_async_copy(_row(yb_ref, dests[0, 0, TOP_K * t + k]),
                                      _row(buf_ref.at[sl, k], t), sem.at[sl]).start()
            return c

        lax.fori_loop(0, tb, issue, 0, unroll=4)

    @pl.when(i == 0)
    def _():
        gather(dest_ref, 0)

    @pl.when(i + 1 < n)
    def _():
        gather(dest_next_ref, 1 - slot)

    for k in range(TOP_K):
        pltpu.make_async_copy(yb_ref.at[pl.ds(0, tb * ROW_CHUNKS), :], buf_ref.at[slot, k],
                              sem.at[slot]).wait()
    rw = rw_ref[...]
    acc = x1_ref[...]
    for k in range(TOP_K):
        y = jnp.concatenate([buf_ref[slot, k, pl.ds(s, tb, stride=ROW_CHUNKS), :]
                             for s in range(ROW_CHUNKS)], axis=-1)
        acc = acc + y * rw[:, k:k + 1]
    o_ref[...] = acc


def _combine(dest3, yb, x1, rw):
    T = x1.shape[0]
    tb = ROUTE_ROWS
    nblk = T // tb
    smem = lambda f: pl.BlockSpec((1, 1, TOP_K * tb), f, memory_space=pltpu.SMEM)
    return pl.pallas_call(
        _combine_kernel,
        out_shape=jax.ShapeDtypeStruct((T, D_MODEL), F32),
        grid=(nblk,),
        in_specs=[smem(lambda i: (i, 0, 0)), smem(lambda i: (jnp.minimum(i + 1, nblk - 1), 0, 0)),
                  pl.BlockSpec(memory_space=pl.ANY),
                  pl.BlockSpec((tb, D_MODEL), lambda i: (i, 0)),
                  pl.BlockSpec((tb, SUBLANES), lambda i: (i, 0))],
        out_specs=pl.BlockSpec((tb, D_MODEL), lambda i: (i, 0)),
        scratch_shapes=[pltpu.VMEM((2, TOP_K, tb * ROW_CHUNKS, LANES), F32),
                        pltpu.SemaphoreType.DMA((2,))],
        compiler_params=pltpu.CompilerParams(dimension_semantics=("arbitrary",),
                                             vmem_limit_bytes=VMEM_LIMIT),
        name="combine")(dest3, dest3, yb, x1, rw)


def _rope_tables(positions, dim, half_lanes, one_fill):
    inv = ROPE_THETA ** (-jnp.arange(0, dim, 2, dtype=F32) / dim)
    ang = positions.astype(F32).reshape(-1, 1) * inv
    c, s = jnp.cos(ang), jnp.sin(ang)
    T = ang.shape[0]
    fill = half_lanes - dim
    cseg = jnp.concatenate([c, c, jnp.full((T, fill), one_fill, F32)], axis=-1)
    sseg = jnp.concatenate([-s, s, jnp.zeros((T, fill), F32)], axis=-1)
    reps = LANES // half_lanes
    return jnp.tile(cseg, (1, reps)), jnp.tile(sseg, (1, reps))


def _layer(x2, B, S, tables, lambda_init, attn_norm_g, w_in, b_gate, da_q_norm_g, da_k_norm_g,
           da_lambda_q1, da_lambda_k1, da_lambda_q2, da_lambda_k2, da_subln_g, mla_q_lora_g,
           mla_w_uq, mla_kv_lora_g, mla_w_ukv, mla_q_norm_g, mla_k_nope_norm_g, mla_k_rope_norm_g,
           w_o, ffn_norm_g, w_group, b_group, w_router, b_router, w1, w3, w2):
    T = B * S
    D = D_MODEL
    c0, c1, c2 = D, 2 * D, 3 * D
    c3 = c2 + MLA_Q_LORA
    c4 = c3 + MLA_KV_LORA
    c5 = c4 + MLA_ROPE
    wb = w_in.astype(BF16)
    wkr = jnp.pad(wb[:, c4:c5], ((0, 0), (0, LANES - MLA_ROPE)))
    qk = MLA_NOPE + MLA_ROPE
    uq = mla_w_uq.astype(BF16).reshape(MLA_Q_LORA, MLA_HEADS, qk)
    wuq = jnp.concatenate([
        uq[:, :, :MLA_NOPE].reshape(MLA_Q_LORA, D),
        jnp.pad(uq[:, :, MLA_NOPE:], ((0, 0), (0, 0), (0, LANES - MLA_ROPE))).reshape(MLA_Q_LORA, D)],
        axis=1)
    ukv = mla_w_ukv.astype(BF16).reshape(MLA_KV_LORA, MLA_HEADS, MLA_NOPE + MLA_V)
    wukv = jnp.concatenate([ukv[:, :, :MLA_NOPE].reshape(MLA_KV_LORA, D),
                            ukv[:, :, MLA_NOPE:].reshape(MLA_KV_LORA, D)], axis=1)
    wts = (wb[:, :c0], wb[:, c0:c1], wb[:, c1:c2], wb[:, c2:c3], wb[:, c3:c4], wkr, wb[:, c5:],
           wuq, wukv)
    r1 = lambda v: v.astype(F32).reshape(1, -1)
    pad_rope = lambda v: jnp.pad(v.astype(F32), (0, LANES - MLA_ROPE)).reshape(1, LANES)
    da_scale = DA_HEAD_DIM ** -0.5 * LOG2_E
    mla_scale = qk ** -0.5 * LOG2_E
    vecs = (r1(b_gate),
            jnp.tile(r1(da_q_norm_g), (1, 2)) * da_scale,
            jnp.tile(r1(da_k_norm_g), (1, 2)),
            r1(mla_q_lora_g), r1(mla_kv_lora_g),
            r1(mla_q_norm_g[:MLA_NOPE]) * mla_scale,
            pad_rope(mla_q_norm_g[MLA_NOPE:]) * mla_scale,
            r1(mla_k_nope_norm_g), pad_rope(mla_k_rope_norm_g))
    q, k, v, qn, qr, kn, vm, kr, gates = _in_proj(x2, r1(attn_norm_g), wts, vecs, tables)

    b3 = lambda a: a.reshape(B, S, a.shape[-1])
    lam_vecs = jnp.stack([da_lambda_q1, da_lambda_k1, da_lambda_q2, da_lambda_k2]).astype(F32)
    o_da = _da_attn(b3(q), b3(k), b3(v), lam_vecs, r1(da_subln_g), lambda_init)
    o_mla = _mla_attn(b3(qn), b3(qr), b3(kn), b3(kr), b3(vm))

    wgr = jnp.concatenate([w_group.astype(F32), w_router.astype(F32),
                           jnp.zeros((D, LANES - N_GROUPS - N_EXPERTS), F32)], axis=1)
    whi = wgr.astype(BF16)
    wlo = (wgr - whi.astype(F32)).astype(BF16)
    br = jnp.concatenate([b_group.astype(F32), b_router.astype(F32),
                          jnp.zeros((LANES - N_GROUPS - N_EXPERTS,), F32)]).reshape(1, LANES)
    x1, h2r, ri, rw, cnt = _merge(o_da.reshape(T, D), o_mla.reshape(T, D), gates, x2,
                                  w_o.astype(BF16), r1(ffn_norm_g), whi, wlo, br)

    ch = MOE_CHUNK
    counts = cnt[0, :N_EXPERTS].astype(jnp.int32)
    padded = (counts + ch - 1) // ch * ch
    pends = jnp.cumsum(padded)
    pstarts = pends - padded
    n_chunks = (T * TOP_K) // ch + N_EXPERTS
    P = n_chunks * ch
    e_ids = jnp.arange(N_EXPERTS, dtype=jnp.int32)
    seg_start = jnp.sum(jnp.where(ri[:, :TOP_K, None] == e_ids, pstarts, 0), axis=-1)
    dest = seg_start + ri[:, TOP_K:2 * TOP_K]
    dest3 = dest.reshape(T // ROUTE_ROWS, 1, TOP_K * ROUTE_ROWS)
    chunk_start = jnp.arange(n_chunks, dtype=jnp.int32)[:, None] * ch
    chunk_e = jnp.minimum(jnp.sum((pends[None, :] <= chunk_start).astype(jnp.int32), axis=1),
                          N_EXPERTS - 1).astype(jnp.int32)
    n_used = (pends[-1:] // ch).astype(jnp.int32)
    tok = jnp.repeat(jnp.arange(T, dtype=jnp.int32), TOP_K)
    tok3 = jnp.zeros((P,), jnp.int32).at[dest.reshape(-1)].set(
        tok, unique_indices=True, mode='promise_in_bounds').reshape(n_chunks, 1, ch)

    yb = _experts(chunk_e, n_used, tok3, h2r, w1.astype(BF16), w3.astype(BF16), w2.astype(BF16))
    return _combine(dest3, yb, x1, rw)


def kernel(x, positions, attn_norm_g, w_in, b_gate, da_q_norm_g, da_k_norm_g, da_lambda_q1, da_lambda_k1, da_lambda_q2, da_lambda_k2, da_subln_g, mla_q_lora_g, mla_w_uq, mla_kv_lora_g, mla_w_ukv, mla_q_norm_g, mla_k_nope_norm_g, mla_k_rope_norm_g, w_o, ffn_norm_g, w_group, b_group, w_router, b_router, w1, w3, w2):
    B, S, D = x.shape
    assert D == D_MODEL and S % max(DA_TQ, MLA_TQ, ATT_TK) == 0 and (B * S) % INPROJ_ROWS == 0
    tables = (_rope_tables(positions, DA_ROT_DIM, DA_HEAD_DIM, 1.0)
              + _rope_tables(positions, MLA_ROPE, LANES, 1.0))
    x2 = x.reshape(B * S, D)
    per_layer = (attn_norm_g, w_in, b_gate, da_q_norm_g, da_k_norm_g, da_lambda_q1, da_lambda_k1,
                 da_lambda_q2, da_lambda_k2, da_subln_g, mla_q_lora_g, mla_w_uq, mla_kv_lora_g,
                 mla_w_ukv, mla_q_norm_g, mla_k_nope_norm_g, mla_k_rope_norm_g, w_o, ffn_norm_g,
                 w_group, b_group, w_router, b_router, w1, w3, w2)
    for l in range(w_in.shape[0]):
        lambda_init = 0.8 - 0.6 * math.exp(-0.3 * l)
        x2 = _layer(x2, B, S, tables, lambda_init, *[p[l] for p in per_layer])
    return x2.reshape(B, S, D)
```

```python
import functools
import math

import jax
import jax.numpy as jnp
from jax import lax
from jax.experimental import pallas as pl
from jax.experimental.pallas import tpu as pltpu

D_MODEL = 1024
DA_HEADS = 8
DA_HEAD_DIM = 64
DA_V_DIM = 128
DA_ROT_DIM = 16
MLA_HEADS = 8
MLA_Q_LORA = 384
MLA_KV_LORA = 256
MLA_NOPE = 128
MLA_ROPE = 64
MLA_V = 128
ROPE_THETA = 500000.0
N_GROUPS = 4
EXPERTS_PER_GROUP = 8
N_EXPERTS = N_GROUPS * EXPERTS_PER_GROUP
TOP_K = 2
D_EXPERT = 512
NORM_EPS = 1e-6
LOG2_E = math.log2(math.e)

LANES = 128
SUBLANES = 8
ROW_CHUNKS = D_MODEL // LANES

INPROJ_ROWS = 512
PROJ_ROWS = 512
DA_TQ = 512
MLA_TQ = 1024
ATT_TK = 512
MOE_CHUNK = 256
ROUTE_ROWS = 256
VMEM_LIMIT = 48 * 1024 * 1024
INPROJ_VMEM_LIMIT = 56 * 1024 * 1024

F32 = jnp.float32
BF16 = jnp.bfloat16


def _const_spec(shape):
    nd = len(shape)
    return pl.BlockSpec(shape, lambda *_: (0,) * nd, pipeline_mode=pl.Buffered(1))


def _rms(v, n):
    return lax.rsqrt(jnp.sum(v * v, axis=-1, keepdims=True) * (1.0 / n) + NORM_EPS)


def _dot(a, b):
    return jnp.dot(a, b, preferred_element_type=F32)


def _dot_nt(a, b):
    return lax.dot_general(a, b, (((1,), (1,)), ((), ())), preferred_element_type=F32)


def _in_proj_kernel(x_ref, g_ref, wq_ref, wk_ref, wv_ref, wcq_ref, wckv_ref, wkr_ref, wg_ref,
                    wuq_ref, wukv_ref, bg_ref, qg_ref, kg_ref, qlg_ref, kvlg_ref, mqn_ref, mqr_ref,
                    mkn_ref, mkr_ref, cda_ref, sda_ref, cm_ref, sm_ref,
                    q_ref, k_ref, v_ref, qn_ref, qr_ref, kn_ref, vm_ref, kr_ref, gate_ref):
    x = x_ref[...]
    h = (x * _rms(x, D_MODEL) * g_ref[...]).astype(BF16)
    rows = x.shape[0]
    lane = lax.broadcasted_iota(jnp.int32, (rows, LANES), 1)
    lo = lane < DA_HEAD_DIM
    da_first = (lane & (DA_HEAD_DIM - 1)) < (DA_ROT_DIM // 2)
    m_first = lane < (MLA_ROPE // 2)
    cda, sda = cda_ref[...], sda_ref[...]
    cm, sm = cm_ref[...], sm_ref[...]

    def da_rope(y):
        partner = jnp.where(da_first, pltpu.roll(y, LANES - DA_ROT_DIM // 2, 1),
                            pltpu.roll(y, DA_ROT_DIM // 2, 1))
        return y * cda + partner * sda

    def mla_rope(y):
        partner = jnp.where(m_first, pltpu.roll(y, LANES - MLA_ROPE // 2, 1),
                            pltpu.roll(y, MLA_ROPE // 2, 1))
        return y * cm + partner * sm

    for w_ref, gain_ref, o_ref in ((wq_ref, qg_ref, q_ref), (wk_ref, kg_ref, k_ref)):
        p_all = _dot(h, w_ref[...])
        gain = gain_ref[...]
        for hd in range(DA_HEADS):
            sl = slice(hd * LANES, (hd + 1) * LANES)
            p = p_all[:, sl]
            p2 = p * p
            s_all = jnp.sum(p2, axis=-1, keepdims=True)
            s_lo = jnp.sum(jnp.where(lo, p2, 0.0), axis=-1, keepdims=True)
            inv = jnp.where(lo, lax.rsqrt(s_lo * (1.0 / DA_HEAD_DIM) + NORM_EPS),
                            lax.rsqrt((s_all - s_lo) * (1.0 / DA_HEAD_DIM) + NORM_EPS))
            o_ref[:, sl] = da_rope(p * inv * gain).astype(BF16)
    v_ref[...] = _dot(h, wv_ref[...]).astype(BF16)

    cq = _dot(h, wcq_ref[...])
    cqn = (cq * _rms(cq, MLA_Q_LORA) * qlg_ref[...]).astype(BF16)
    qm = _dot(cqn, wuq_ref[...])
    gn, gr = mqn_ref[...], mqr_ref[...]
    for hd in range(MLA_HEADS):
        sl = slice(hd * LANES, (hd + 1) * LANES)
        qn = qm[:, sl]
        qr = qm[:, D_MODEL + hd * LANES:D_MODEL + (hd + 1) * LANES]
        ss = jnp.sum(qn * qn, axis=-1, keepdims=True) + jnp.sum(qr * qr, axis=-1, keepdims=True)
        inv = lax.rsqrt(ss * (1.0 / (MLA_NOPE + MLA_ROPE)) + NORM_EPS)
        qn_ref[:, sl] = (qn * inv * gn).astype(BF16)
        qr_ref[:, sl] = mla_rope(qr * inv * gr).astype(BF16)

    ckv = _dot(h, wckv_ref[...])
    ckvn = (ckv * _rms(ckv, MLA_KV_LORA) * kvlg_ref[...]).astype(BF16)
    kv = _dot(ckvn, wukv_ref[...])
    gkn = mkn_ref[...]
    for hd in range(MLA_HEADS):
        sl = slice(hd * LANES, (hd + 1) * LANES)
        kn = kv[:, sl]
        kn_ref[:, sl] = (kn * _rms(kn, MLA_NOPE) * gkn).astype(BF16)
    vm_ref[...] = kv[:, D_MODEL:].astype(BF16)
    kr = _dot(h, wkr_ref[...])
    kr_ref[...] = mla_rope(kr * _rms(kr, MLA_ROPE) * mkr_ref[...]).astype(BF16)

    gl = _dot(h, wg_ref[...]) + bg_ref[...]
    gate_ref[...] = (1.0 / (1.0 + jnp.exp(-gl))).astype(BF16)


def _in_proj(x2, g, wts, vecs, tables):
    T = x2.shape[0]
    tm = INPROJ_ROWS
    row = lambda w: pl.BlockSpec((tm, w), lambda i: (i, 0))
    in_specs = ([row(D_MODEL), _const_spec(g.shape)] + [_const_spec(w.shape) for w in wts]
                + [_const_spec(v.shape) for v in vecs] + [row(LANES)] * 4)
    outs = [jax.ShapeDtypeStruct((T, D_MODEL), BF16)] * 7 + [
        jax.ShapeDtypeStruct((T, LANES), BF16), jax.ShapeDtypeStruct((T, 2 * D_MODEL), BF16)]
    out_specs = [row(D_MODEL)] * 7 + [row(LANES), row(2 * D_MODEL)]
    return pl.pallas_call(
        _in_proj_kernel, out_shape=outs, grid=(T // tm,), in_specs=in_specs, out_specs=out_specs,
        compiler_params=pltpu.CompilerParams(dimension_semantics=("parallel",),
                                             vmem_limit_bytes=INPROJ_VMEM_LIMIT),
        name="in_proj")(x2, g, *wts, *vecs, *tables)


def _flash(qs, load_k, load_v, qi, tq, n_maps, sa_ref, sb_ref, m_ref, acc_ref):
    rows = n_maps * tq
    tk = ATT_TK
    assert tk % tq == 0 or tq == 2 * tk
    m_ref[...] = jnp.full(m_ref.shape, -jnp.inf, F32)
    acc_ref[...] = jnp.zeros(acc_ref.shape, F32)

    def scores(j, s_ref):
        s_ref[...] = _dot_nt(qs, load_k(j))

    def update(j, s_ref, masked):
        s = s_ref[...]
        if masked:
            qpos = qi * tq + (lax.broadcasted_iota(jnp.int32, (rows, tk), 0) & (tq - 1))
            kpos = j * tk + lax.broadcasted_iota(jnp.int32, (rows, tk), 1)
            s = jnp.where(kpos <= qpos, s, -jnp.inf)
        m_old = m_ref[...]
        m_new = jnp.maximum(m_old, jnp.max(s, axis=-1, keepdims=True))
        alpha = jnp.exp2(m_old - m_new)
        p = jnp.exp2(s - m_new).astype(BF16)
        v = load_v(j)
        acc_ref[...] = alpha * acc_ref[...] + _dot(p, jnp.concatenate([v, jnp.ones_like(v)], axis=1))
        m_ref[...] = m_new

    n_full = (qi * tq) // tk
    n_pairs = n_full // 2

    scores(0, sa_ref)

    def body(i, carry):
        scores(2 * i + 1, sb_ref)
        update(2 * i, sa_ref, False)
        scores(2 * i + 2, sa_ref)
        update(2 * i + 1, sb_ref, False)
        return carry

    lax.fori_loop(0, n_pairs, body, 0)
    e0 = 2 * n_pairs

    if tq == 2 * tk:
        scores(e0 + 1, sb_ref)
        update(e0, sa_ref, True)
        update(e0 + 1, sb_ref, True)
    else:
        tail2 = n_full - e0

        @pl.when(tail2 == 1)
        def _():
            scores(e0 + 1, sb_ref)
            update(e0, sa_ref, False)
            update(e0 + 1, sb_ref, True)

        @pl.when(tail2 == 0)
        def _():
            update(e0, sa_ref, True)

    return acc_ref[:, :LANES], acc_ref[:, LANES:LANES + 1]


def _flash_scratch(rows):
    return [pltpu.VMEM((rows, ATT_TK), F32), pltpu.VMEM((rows, ATT_TK), F32),
            pltpu.VMEM((rows, 1), F32), pltpu.VMEM((rows, 2 * LANES), F32)]


def _kv_block(j):
    return pl.ds(pl.multiple_of(j * ATT_TK, ATT_TK), ATT_TK)


def _da_attn_kernel(q_ref, k_ref, v_ref, lam_ref, sg_ref, o_ref, *scratch, lambda_init):
    qi = pl.program_id(2)
    tq = DA_TQ
    q = q_ref[...]
    lane = lax.broadcasted_iota(jnp.int32, q.shape, 1)
    zero = jnp.zeros_like(q)
    qs = jnp.concatenate([jnp.where(lane < DA_HEAD_DIM, q, zero),
                          jnp.where(lane >= DA_HEAD_DIM, q, zero)], axis=0)

    acc, l = _flash(qs, lambda j: k_ref[_kv_block(j), :], lambda j: v_ref[_kv_block(j), :],
                    qi, tq, 2, *scratch)
    o1 = acc[:tq] / l[:tq]
    o2 = acc[tq:] / l[tq:]
    lv = lam_ref[...]
    lam = (jnp.exp(jnp.sum(lv[0:1] * lv[1:2], axis=-1, keepdims=True))
           - jnp.exp(jnp.sum(lv[2:3] * lv[3:4], axis=-1, keepdims=True)) + lambda_init)
    o = o1 - lam * o2
    y = o * _rms(o, DA_V_DIM) * sg_ref[...] * (1.0 - lambda_init)
    o_ref[...] = y.astype(BF16)


def _da_attn(q, k, v, lam_vecs, subln_g, lambda_init):
    B, S, _ = q.shape
    tq = DA_TQ
    qspec = pl.BlockSpec((None, tq, LANES), lambda b, h, i: (b, i, h))
    kvspec = pl.BlockSpec((None, S, LANES), lambda b, h, i: (b, 0, h))
    return pl.pallas_call(
        functools.partial(_da_attn_kernel, lambda_init=lambda_init),
        out_shape=jax.ShapeDtypeStruct((B, S, D_MODEL), BF16),
        grid=(B, DA_HEADS, S // tq),
        in_specs=[qspec, kvspec, kvspec, _const_spec(lam_vecs.shape), _const_spec(subln_g.shape)],
        out_specs=qspec,
        scratch_shapes=_flash_scratch(2 * tq),
        compiler_params=pltpu.CompilerParams(
            dimension_semantics=("parallel", "parallel", "arbitrary"), vmem_limit_bytes=VMEM_LIMIT),
        name="da_attn")(q, k, v, lam_vecs, subln_g)


def _mla_attn_kernel(qn_ref, qr_ref, kn_ref, kr_ref, v_ref, o_ref, *scratch):
    qi = pl.program_id(2)
    qs = jnp.concatenate([qn_ref[...], qr_ref[...]], axis=-1)

    def load_k(j):
        return jnp.concatenate([kn_ref[_kv_block(j), :], kr_ref[_kv_block(j), :]], axis=-1)

    acc, l = _flash(qs, load_k, lambda j: v_ref[_kv_block(j), :], qi, MLA_TQ, 1, *scratch)
    o_ref[...] = (acc / l).astype(BF16)


def _mla_attn(qn, qr, kn, kr, vm):
    B, S, _ = qn.shape
    tq = MLA_TQ
    qspec = pl.BlockSpec((None, tq, LANES), lambda b, h, i: (b, i, h))
    kvspec = pl.BlockSpec((None, S, LANES), lambda b, h, i: (b, 0, h))
    krspec = pl.BlockSpec((None, S, LANES), lambda b, h, i: (b, 0, 0))
    return pl.pallas_call(
        _mla_attn_kernel,
        out_shape=jax.ShapeDtypeStruct((B, S, D_MODEL), BF16),
        grid=(B, MLA_HEADS, S // tq),
        in_specs=[qspec, qspec, kvspec, krspec, kvspec],
        out_specs=qspec,
        scratch_shapes=_flash_scratch(tq),
        compiler_params=pltpu.CompilerParams(
            dimension_semantics=("parallel", "parallel", "arbitrary"), vmem_limit_bytes=VMEM_LIMIT),
        name="mla_attn")(qn, qr, kn, kr, vm)


def _merge_kernel(oda_ref, omla_ref, gate_ref, x_ref, wo_ref, fg_ref, whi_ref, wlo_ref, br_ref,
                  x1_ref, h2_ref, ri_ref, rw_ref, cnt_ref, carry_ref):
    i = pl.program_id(0)
    rows = x_ref.shape[0]

    @pl.when(i == 0)
    def _():
        carry_ref[...] = jnp.zeros_like(carry_ref)

    g = gate_ref[...].astype(F32)
    mixed = g[:, :D_MODEL] * oda_ref[...].astype(F32) + g[:, D_MODEL:] * omla_ref[...].astype(F32)
    x1 = x_ref[...] + _dot(mixed.astype(BF16), wo_ref[...])
    x1_ref[...] = x1
    h2 = x1 * _rms(x1, D_MODEL) * fg_ref[...]
    for s in range(ROW_CHUNKS):
        h2_ref[pl.ds(s, rows, stride=ROW_CHUNKS), :] = h2[:, s * LANES:(s + 1) * LANES]

    hi = h2.astype(BF16)
    lo = (h2 - hi.astype(F32)).astype(BF16)
    whi = whi_ref[...]
    logits = _dot(hi, whi) + _dot(hi, wlo_ref[...]) + _dot(lo, whi) + br_ref[...]

    lane = lax.broadcasted_iota(jnp.int32, (rows, LANES), 1)
    neg = -jnp.inf
    gl = jnp.where(lane < N_GROUPS, logits, neg)
    gmax = jnp.max(gl, axis=-1, keepdims=True)
    g_sel = jnp.min(jnp.where(gl == gmax, lane, LANES), axis=-1, keepdims=True)
    p_g = 1.0 / jnp.sum(jnp.exp(gl - gmax), axis=-1, keepdims=True)
    e_lo = N_GROUPS + g_sel * EXPERTS_PER_GROUP
    el = jnp.where((lane >= e_lo) & (lane < e_lo + EXPERTS_PER_GROUP), logits, neg)
    v0 = jnp.max(el, axis=-1, keepdims=True)
    i0 = jnp.min(jnp.where(el == v0, lane, LANES), axis=-1, keepdims=True)
    el2 = jnp.where(lane == i0, neg, el)
    v1 = jnp.max(el2, axis=-1, keepdims=True)
    i1 = jnp.min(jnp.where(el2 == v1, lane, LANES), axis=-1, keepdims=True)
    t = jnp.exp(v1 - v0)
    w0 = p_g / (1.0 + t)
    w1 = p_g * t / (1.0 + t)
    e0 = i0 - N_GROUPS
    e1 = i1 - N_GROUPS

    oh = jnp.where((lane == e0) | (lane == e1), 1.0, 0.0)
    r_i = lax.broadcasted_iota(jnp.int32, (rows, rows), 0)
    c_i = lax.broadcasted_iota(jnp.int32, (rows, rows), 1)
    tri = jnp.where(c_i < r_i, 1.0, 0.0).astype(BF16)
    before = _dot(tri, oh.astype(BF16)) + carry_ref[...]
    rank0 = jnp.sum(jnp.where(lane == e0, before, 0.0), axis=-1, keepdims=True).astype(jnp.int32)
    rank1 = jnp.sum(jnp.where(lane == e1, before, 0.0), axis=-1, keepdims=True).astype(jnp.int32)
    carry = carry_ref[...] + jnp.sum(oh, axis=0, keepdims=True)
    carry_ref[...] = carry
    cnt_ref[...] = carry

    ri = jnp.where(lane == 0, e0, jnp.where(lane == 1, e1, jnp.where(lane == 2, rank0, rank1)))
    ri_ref[...] = ri[:, :SUBLANES]
    rw_ref[...] = jnp.where(lane == 0, w0, w1)[:, :SUBLANES]


def _merge(oda, omla, gates, x2, wo, fg, whi, wlo, br):
    T = x2.shape[0]
    tm = PROJ_ROWS
    row = lambda w: pl.BlockSpec((tm, w), lambda i: (i, 0))
    outs = [jax.ShapeDtypeStruct((T, D_MODEL), F32),
            jax.ShapeDtypeStruct((T * ROW_CHUNKS, LANES), F32),
            jax.ShapeDtypeStruct((T, SUBLANES), jnp.int32),
            jax.ShapeDtypeStruct((T, SUBLANES), F32),
            jax.ShapeDtypeStruct((1, LANES), F32)]
    out_specs = [row(D_MODEL), pl.BlockSpec((tm * ROW_CHUNKS, LANES), lambda i: (i, 0)),
                 row(SUBLANES), row(SUBLANES), pl.BlockSpec((1, LANES), lambda i: (0, 0))]
    return pl.pallas_call(
        _merge_kernel, out_shape=outs, grid=(T // tm,),
        in_specs=[row(D_MODEL), row(D_MODEL), row(2 * D_MODEL), row(D_MODEL), _const_spec(wo.shape),
                  _const_spec(fg.shape), _const_spec(whi.shape), _const_spec(wlo.shape),
                  _const_spec(br.shape)],
        out_specs=out_specs,
        scratch_shapes=[pltpu.VMEM((1, LANES), F32)],
        compiler_params=pltpu.CompilerParams(dimension_semantics=("arbitrary",),
                                             vmem_limit_bytes=VMEM_LIMIT),
        name="merge_router")(oda, omla, gates, x2, wo, fg, whi, wlo, br)


def _row(ref, r):
    return ref.at[pl.ds(pl.multiple_of(r * ROW_CHUNKS, ROW_CHUNKS), ROW_CHUNKS), :]


def _expert_kernel(ce_ref, nu_ref, tok_ref, tok_next_ref, h2_ref, w1_ref, w3_ref, w2_ref, yb_ref,
                   buf_ref, sem):
    c = pl.program_id(0)
    ch = MOE_CHUNK
    slot = c & 1
    n_used = nu_ref[0]

    def row_copy(toks, sl, t):
        return pltpu.make_async_copy(_row(h2_ref, toks[0, 0, t]), _row(buf_ref.at[sl], t),
                                     sem.at[sl])

    def slot_wait(sl):
        pltpu.make_async_copy(h2_ref.at[pl.ds(0, ch * ROW_CHUNKS), :], buf_ref.at[sl],
                              sem.at[sl]).wait()

    @pl.when(c == 0)
    def _():
        def issue(t, carry):
            row_copy(tok_ref, 0, t).start()
            return carry

        lax.fori_loop(0, ch, issue, 0, unroll=8)

    @pl.when(c < n_used)
    def _():
        slot_wait(slot)
        for t in range(ch):
            row_copy(tok_next_ref, 1 - slot, t).start()
        x = jnp.concatenate([buf_ref[slot, pl.ds(s, ch, stride=ROW_CHUNKS), :]
                             for s in range(ROW_CHUNKS)], axis=-1).astype(BF16)
        a = _dot(x, w1_ref[...])
        b = _dot(x, w3_ref[...])
        hmid = (a / (1.0 + jnp.exp(-a))) * b
        y = _dot(hmid.astype(BF16), w2_ref[...])
        for s in range(ROW_CHUNKS):
            yb_ref[pl.ds(s, ch, stride=ROW_CHUNKS), :] = y[:, s * LANES:(s + 1) * LANES]

    @pl.when(c >= n_used)
    def _():
        yb_ref[...] = jnp.zeros_like(yb_ref)

    @pl.when(c == n_used)
    def _():
        slot_wait(slot)


def _experts(chunk_e, n_used, tok3, h2r, w1, w3, w2):
    ch = MOE_CHUNK
    n_chunks = tok3.shape[0]
    smem = lambda f: pl.BlockSpec((1, 1, ch), f, memory_space=pltpu.SMEM)
    wspec = lambda s: pl.BlockSpec((None,) + s, lambda c, ce, nu: (ce[c], 0, 0))
    return pl.pallas_call(
        _expert_kernel,
        out_shape=jax.ShapeDtypeStruct((n_chunks * ch * ROW_CHUNKS, LANES), F32),
        grid_spec=pltpu.PrefetchScalarGridSpec(
            num_scalar_prefetch=2, grid=(n_chunks,),
            in_specs=[smem(lambda c, ce, nu: (c, 0, 0)),
                      smem(lambda c, ce, nu: (jnp.minimum(c + 1, n_chunks - 1), 0, 0)),
                      pl.BlockSpec(memory_space=pl.ANY),
                      wspec((D_MODEL, D_EXPERT)), wspec((D_MODEL, D_EXPERT)),
                      wspec((D_EXPERT, D_MODEL))],
            out_specs=pl.BlockSpec((ch * ROW_CHUNKS, LANES), lambda c, ce, nu: (c, 0)),
            scratch_shapes=[pltpu.VMEM((2, ch * ROW_CHUNKS, LANES), F32),
                            pltpu.SemaphoreType.DMA((2,))]),
        compiler_params=pltpu.CompilerParams(dimension_semantics=("arbitrary",),
                                             vmem_limit_bytes=VMEM_LIMIT),
        name="experts")(chunk_e, n_used, tok3, tok3, h2r, w1, w3, w2)


def _combine_kernel(dest_ref, dest_next_ref, yb_ref, x1_ref, rw_ref, o_ref, buf_ref, sem):
    i = pl.program_id(0)
    n = pl.num_programs(0)
    tb = ROUTE_ROWS
    slot = i & 1

    def gather(dests, sl):
        def issue(t, c):
            for k in range(TOP_K):
                pltpu.make_async_copy(_row(yb_ref, dests[0, 0, TOP_K * t + k]),
                                      _row(buf_ref.at[sl, k], t), sem.at[sl]).start()
            return c

        lax.fori_loop(0, tb, issue, 0, unroll=4)

    @pl.when(i == 0)
    def _():
        gather(dest_ref, 0)

    @pl.when(i + 1 < n)
    def _():
        gather(dest_next_ref, 1 - slot)

    for k in range(TOP_K):
        pltpu.make_async_copy(yb_ref.at[pl.ds(0, tb * ROW_CHUNKS), :], buf_ref.at[slot, k],
                              sem.at[slot]).wait()
    rw = rw_ref[...]
    acc = x1_ref[...]
    for k in range(TOP_K):
        y = jnp.concatenate([buf_ref[slot, k, pl.ds(s, tb, stride=ROW_CHUNKS), :]
                             for s in range(ROW_CHUNKS)], axis=-1)
        acc = acc + y * rw[:, k:k + 1]
    o_ref[...] = acc


def _combine(dest3, yb, x1, rw):
    T = x1.shape[0]
    tb = ROUTE_ROWS
    nblk = T // tb
    smem = lambda f: pl.BlockSpec((1, 1, TOP_K * tb), f, memory_space=pltpu.SMEM)
    return pl.pallas_call(
        _combine_kernel,
        out_shape=jax.ShapeDtypeStruct((T, D_MODEL), F32),
        grid=(nblk,),
        in_specs=[smem(lambda i: (i, 0, 0)), smem(lambda i: (jnp.minimum(i + 1, nblk - 1), 0, 0)),
                  pl.BlockSpec(memory_space=pl.ANY),
                  pl.BlockSpec((tb, D_MODEL), lambda i: (i, 0)),
                  pl.BlockSpec((tb, SUBLANES), lambda i: (i, 0))],
        out_specs=pl.BlockSpec((tb, D_MODEL), lambda i: (i, 0)),
        scratch_shapes=[pltpu.VMEM((2, TOP_K, tb * ROW_CHUNKS, LANES), F32),
                        pltpu.SemaphoreType.DMA((2,))],
        compiler_params=pltpu.CompilerParams(dimension_semantics=("arbitrary",),
                                             vmem_limit_bytes=VMEM_LIMIT),
        name="combine")(dest3, dest3, yb, x1, rw)


def _rope_tables(positions, dim, half_lanes, one_fill):
    inv = ROPE_THETA ** (-jnp.arange(0, dim, 2, dtype=F32) / dim)
    ang = positions.astype(F32).reshape(-1, 1) * inv
    c, s = jnp.cos(ang), jnp.sin(ang)
    T = ang.shape[0]
    fill = half_lanes - dim
    cseg = jnp.concatenate([c, c, jnp.full((T, fill), one_fill, F32)], axis=-1)
    sseg = jnp.concatenate([-s, s, jnp.zeros((T, fill), F32)], axis=-1)
    reps = LANES // half_lanes
    return jnp.tile(cseg, (1, reps)), jnp.tile(sseg, (1, reps))


def _layer(x2, B, S, tables, lambda_init, attn_norm_g, w_in, b_gate, da_q_norm_g, da_k_norm_g,
           da_lambda_q1, da_lambda_k1, da_lambda_q2, da_lambda_k2, da_subln_g, mla_q_lora_g,
           mla_w_uq, mla_kv_lora_g, mla_w_ukv, mla_q_norm_g, mla_k_nope_norm_g, mla_k_rope_norm_g,
           w_o, ffn_norm_g, w_group, b_group, w_router, b_router, w1, w3, w2):
    T = B * S
    D = D_MODEL
    c0, c1, c2 = D, 2 * D, 3 * D
    c3 = c2 + MLA_Q_LORA
    c4 = c3 + MLA_KV_LORA
    c5 = c4 + MLA_ROPE
    wb = w_in.astype(BF16)
    wkr = jnp.pad(wb[:, c4:c5], ((0, 0), (0, LANES - MLA_ROPE)))
    qk = MLA_NOPE + MLA_ROPE
    uq = mla_w_uq.astype(BF16).reshape(MLA_Q_LORA, MLA_HEADS, qk)
    wuq = jnp.concatenate([
        uq[:, :, :MLA_NOPE].reshape(MLA_Q_LORA, D),
        jnp.pad(uq[:, :, MLA_NOPE:], ((0, 0), (0, 0), (0, LANES - MLA_ROPE))).reshape(MLA_Q_LORA, D)],
        axis=1)
    ukv = mla_w_ukv.astype(BF16).reshape(MLA_KV_LORA, MLA_HEADS, MLA_NOPE + MLA_V)
    wukv = jnp.concatenate([ukv[:, :, :MLA_NOPE].reshape(MLA_KV_LORA, D),
                            ukv[:, :, MLA_NOPE:].reshape(MLA_KV_LORA, D)], axis=1)
    wts = (wb[:, :c0], wb[:, c0:c1], wb[:, c1:c2], wb[:, c2:c3], wb[:, c3:c4], wkr, wb[:, c5:],
           wuq, wukv)
    r1 = lambda v: v.astype(F32).reshape(1, -1)
    pad_rope = lambda v: jnp.pad(v.astype(F32), (0, LANES - MLA_ROPE)).reshape(1, LANES)
    da_scale = DA_HEAD_DIM ** -0.5 * LOG2_E
    mla_scale = qk ** -0.5 * LOG2_E
    vecs = (r1(b_gate),
            jnp.tile(r1(da_q_norm_g), (1, 2)) * da_scale,
            jnp.tile(r1(da_k_norm_g), (1, 2)),
            r1(mla_q_lora_g), r1(mla_kv_lora_g),
            r1(mla_q_norm_g[:MLA_NOPE]) * mla_scale,
            pad_rope(mla_q_norm_g[MLA_NOPE:]) * mla_scale,
            r1(mla_k_nope_norm_g), pad_rope(mla_k_rope_norm_g))
    q, k, v, qn, qr, kn, vm, kr, gates = _in_proj(x2, r1(attn_norm_g), wts, vecs, tables)

    b3 = lambda a: a.reshape(B, S, a.shape[-1])
    lam_vecs = jnp.stack([da_lambda_q1, da_lambda_k1, da_lambda_q2, da_lambda_k2]).astype(F32)
    o_da = _da_attn(b3(q), b3(k), b3(v), lam_vecs, r1(da_subln_g), lambda_init)
    o_mla = _mla_attn(b3(qn), b3(qr), b3(kn), b3(kr), b3(vm))

    wgr = jnp.concatenate([w_group.astype(F32), w_router.astype(F32),
                           jnp.zeros((D, LANES - N_GROUPS - N_EXPERTS), F32)], axis=1)
    whi = wgr.astype(BF16)
    wlo = (wgr - whi.astype(F32)).astype(BF16)
    br = jnp.concatenate([b_group.astype(F32), b_router.astype(F32),
                          jnp.zeros((LANES - N_GROUPS - N_EXPERTS,), F32)]).reshape(1, LANES)
    x1, h2r, ri, rw, cnt = _merge(o_da.reshape(T, D), o_mla.reshape(T, D), gates, x2,
                                  w_o.astype(BF16), r1(ffn_norm_g), whi, wlo, br)

    ch = MOE_CHUNK
    counts = cnt[0, :N_EXPERTS].astype(jnp.int32)
    padded = (counts + ch - 1) // ch * ch
    pends = jnp.cumsum(padded)
    pstarts = pends - padded
    n_chunks = (T * TOP_K) // ch + N_EXPERTS
    P = n_chunks * ch
    e_ids = jnp.arange(N_EXPERTS, dtype=jnp.int32)
    seg_start = jnp.sum(jnp.where(ri[:, :TOP_K, None] == e_ids, pstarts, 0), axis=-1)
    dest = seg_start + ri[:, TOP_K:2 * TOP_K]
    dest3 = dest.reshape(T // ROUTE_ROWS, 1, TOP_K * ROUTE_ROWS)
    chunk_start = jnp.arange(n_chunks, dtype=jnp.int32)[:, None] * ch
    chunk_e = jnp.minimum(jnp.sum((pends[None, :] <= chunk_start).astype(jnp.int32), axis=1),
                          N_EXPERTS - 1).astype(jnp.int32)
    n_used = (pends[-1:] // ch).astype(jnp.int32)
    tok = jnp.repeat(jnp.arange(T, dtype=jnp.int32), TOP_K)
    tok3 = jnp.zeros((P,), jnp.int32).at[dest.reshape(-1)].set(
        tok, unique_indices=True, mode='promise_in_bounds').reshape(n_chunks, 1, ch)

    yb = _experts(chunk_e, n_used, tok3, h2r, w1.astype(BF16), w3.astype(BF16), w2.astype(BF16))
    return _combine(dest3, yb, x1, rw)


def kernel(x, positions, attn_norm_g, w_in, b_gate, da_q_norm_g, da_k_norm_g, da_lambda_q1, da_lambda_k1, da_lambda_q2, da_lambda_k2, da_subln_g, mla_q_lora_g, mla_w_uq, mla_kv_lora_g, mla_w_ukv, mla_q_norm_g, mla_k_nope_norm_g, mla_k_rope_norm_g, w_o, ffn_norm_g, w_group, b_group, w_router, b_router, w1, w3, w2):
    B, S, D = x.shape
    assert D == D_MODEL and S % max(DA_TQ, MLA_TQ, ATT_TK) == 0 and (B * S) % INPROJ_ROWS == 0
    tables = (_rope_tables(positions, DA_ROT_DIM, DA_HEAD_DIM, 1.0)
              + _rope_tables(positions, MLA_ROPE, LANES, 1.0))
    x2 = x.reshape(B * S, D)
    per_layer = (attn_norm_g, w_in, b_gate, da_q_norm_g, da_k_norm_g, da_lambda_q1, da_lambda_k1,
                 da_lambda_q2, da_lambda_k2, da_subln_g, mla_q_lora_g, mla_w_uq, mla_kv_lora_g,
                 mla_w_ukv, mla_q_norm_g, mla_k_nope_norm_g, mla_k_rope_norm_g, w_o, ffn_norm_g,
                 w_group, b_group, w_router, b_router, w1, w3, w2)
    for l in range(w_in.shape[0]):
        lambda_init = 0.8 - 0.6 * math.exp(-0.3 * l)
        x2 = _layer(x2, B, S, tables, lambda_init, *[p[l] for p in per_layer])
    return x2.reshape(B, S, D)
```

```python
import functools
import math

import jax
import jax.numpy as jnp
from jax import lax
from jax.experimental import pallas as pl
from jax.experimental.pallas import tpu as pltpu

D_MODEL = 1024
DA_HEADS = 8
DA_HEAD_DIM = 64
DA_V_DIM = 128
DA_ROT_DIM = 16
MLA_HEADS = 8
MLA_Q_LORA = 384
MLA_KV_LORA = 256
MLA_NOPE = 128
MLA_ROPE = 64
MLA_V = 128
ROPE_THETA = 500000.0
N_GROUPS = 4
EXPERTS_PER_GROUP = 8
N_EXPERTS = N_GROUPS * EXPERTS_PER_GROUP
TOP_K = 2
D_EXPERT = 512
NORM_EPS = 1e-6
LOG2_E = math.log2(math.e)

LANES = 128
SUBLANES = 8
ROW_CHUNKS = D_MODEL // LANES

INPROJ_ROWS = 512
PROJ_ROWS = 512
DA_TQ = 512
MLA_TQ = 1024
ATT_TK = 512
MOE_CHUNK = 256
ROUTE_ROWS = 256
VMEM_LIMIT = 48 * 1024 * 1024
INPROJ_VMEM_LIMIT = 56 * 1024 * 1024

F32 = jnp.float32
BF16 = jnp.bfloat16


def _const_spec(shape):
    nd = len(shape)
    return pl.BlockSpec(shape, lambda *_: (0,) * nd, pipeline_mode=pl.Buffered(1))


def _rms(v, n):
    return lax.rsqrt(jnp.sum(v * v, axis=-1, keepdims=True) * (1.0 / n) + NORM_EPS)


def _dot(a, b):
    return jnp.dot(a, b, preferred_element_type=F32)


def _dot_nt(a, b):
    return lax.dot_general(a, b, (((1,), (1,)), ((), ())), preferred_element_type=F32)


def _in_proj_kernel(x_ref, g_ref, wq_ref, wk_ref, wv_ref, wcq_ref, wckv_ref, wkr_ref, wg_ref,
                    wuq_ref, wukv_ref, bg_ref, qg_ref, kg_ref, qlg_ref, kvlg_ref, mqn_ref, mqr_ref,
                    mkn_ref, mkr_ref, cda_ref, sda_ref, cm_ref, sm_ref,
                    q_ref, k_ref, v_ref, qn_ref, qr_ref, kn_ref, vm_ref, kr_ref, gate_ref):
    x = x_ref[...]
    h = (x * _rms(x, D_MODEL) * g_ref[...]).astype(BF16)
    rows = x.shape[0]
    lane = lax.broadcasted_iota(jnp.int32, (rows, LANES), 1)
    lo = lane < DA_HEAD_DIM
    da_first = (lane & (DA_HEAD_DIM - 1)) < (DA_ROT_DIM // 2)
    m_first = lane < (MLA_ROPE // 2)
    cda, sda = cda_ref[...], sda_ref[...]
    cm, sm = cm_ref[...], sm_ref[...]

    def da_rope(y):
        partner = jnp.where(da_first, pltpu.roll(y, LANES - DA_ROT_DIM // 2, 1),
                            pltpu.roll(y, DA_ROT_DIM // 2, 1))
        return y * cda + partner * sda

    def mla_rope(y):
        partner = jnp.where(m_first, pltpu.roll(y, LANES - MLA_ROPE // 2, 1),
                            pltpu.roll(y, MLA_ROPE // 2, 1))
        return y * cm + partner * sm

    for w_ref, gain_ref, o_ref in ((wq_ref, qg_ref, q_ref), (wk_ref, kg_ref, k_ref)):
        p_all = _dot(h, w_ref[...])
        gain = gain_ref[...]
        for hd in range(DA_HEADS):
            sl = slice(hd * LANES, (hd + 1) * LANES)
            p = p_all[:, sl]
            p2 = p * p
            s_all = jnp.sum(p2, axis=-1, keepdims=True)
            s_lo = jnp.sum(jnp.where(lo, p2, 0.0), axis=-1, keepdims=True)
            inv = jnp.where(lo, lax.rsqrt(s_lo * (1.0 / DA_HEAD_DIM) + NORM_EPS),
                            lax.rsqrt((s_all - s_lo) * (1.0 / DA_HEAD_DIM) + NORM_EPS))
            o_ref[:, sl] = da_rope(p * inv * gain).astype(BF16)
    v_ref[...] = _dot(h, wv_ref[...]).astype(BF16)

    cq = _dot(h, wcq_ref[...])
    cqn = (cq * _rms(cq, MLA_Q_LORA) * qlg_ref[...]).astype(BF16)
    qm = _dot(cqn, wuq_ref[...])
    gn, gr = mqn_ref[...], mqr_ref[...]
    for hd in range(MLA_HEADS):
        sl = slice(hd * LANES, (hd + 1) * LANES)
        qn = qm[:, sl]
        qr = qm[:, D_MODEL + hd * LANES:D_MODEL + (hd + 1) * LANES]
        ss = jnp.sum(qn * qn, axis=-1, keepdims=True) + jnp.sum(qr * qr, axis=-1, keepdims=True)
        inv = lax.rsqrt(ss * (1.0 / (MLA_NOPE + MLA_ROPE)) + NORM_EPS)
        qn_ref[:, sl] = (qn * inv * gn).astype(BF16)
        qr_ref[:, sl] = mla_rope(qr * inv * gr).astype(BF16)

    ckv = _dot(h, wckv_ref[...])
    ckvn = (ckv * _rms(ckv, MLA_KV_LORA) * kvlg_ref[...]).astype(BF16)
    kv = _dot(ckvn, wukv_ref[...])
    gkn = mkn_ref[...]
    for hd in range(MLA_HEADS):
        sl = slice(hd * LANES, (hd + 1) * LANES)
        kn = kv[:, sl]
        kn_ref[:, sl] = (kn * _rms(kn, MLA_NOPE) * gkn).astype(BF16)
    vm_ref[...] = kv[:, D_MODEL:].astype(BF16)
    kr = _dot(h, wkr_ref[...])
    kr_ref[...] = mla_rope(kr * _rms(kr, MLA_ROPE) * mkr_ref[...]).astype(BF16)

    gl = _dot(h, wg_ref[...]) + bg_ref[...]
    gate_ref[...] = (1.0 / (1.0 + jnp.exp(-gl))).astype(BF16)


def _in_proj(x2, g, wts, vecs, tables):
    T = x2.shape[0]
    tm = INPROJ_ROWS
    row = lambda w: pl.BlockSpec((tm, w), lambda i: (i, 0))
    in_specs = ([row(D_MODEL), _const_spec(g.shape)] + [_const_spec(w.shape) for w in wts]
                + [_const_spec(v.shape) for v in vecs] + [row(LANES)] * 4)
    outs = [jax.ShapeDtypeStruct((T, D_MODEL), BF16)] * 7 + [
        jax.ShapeDtypeStruct((T, LANES), BF16), jax.ShapeDtypeStruct((T, 2 * D_MODEL), BF16)]
    out_specs = [row(D_MODEL)] * 7 + [row(LANES), row(2 * D_MODEL)]
    return pl.pallas_call(
        _in_proj_kernel, out_shape=outs, grid=(T // tm,), in_specs=in_specs, out_specs=out_specs,
        compiler_params=pltpu.CompilerParams(dimension_semantics=("parallel",),
                                             vmem_limit_bytes=INPROJ_VMEM_LIMIT),
        name="in_proj")(x2, g, *wts, *vecs, *tables)


def _flash(qs, load_k, load_v, qi, tq, n_maps, sa_ref, sb_ref, m_ref, acc_ref):
    rows = n_maps * tq
    tk = ATT_TK
    assert tk % tq == 0 or tq == 2 * tk
    m_ref[...] = jnp.full(m_ref.shape, -jnp.inf, F32)
    acc_ref[...] = jnp.zeros(acc_ref.shape, F32)

    def scores(j, s_ref):
        s_ref[...] = _dot_nt(qs, load_k(j))

    def update(j, s_ref, masked):
        s = s_ref[...]
        if masked:
            qpos = qi * tq + (lax.broadcasted_iota(jnp.int32, (rows, tk), 0) & (tq - 1))
            kpos = j * tk + lax.broadcasted_iota(jnp.int32, (rows, tk), 1)
            s = jnp.where(kpos <= qpos, s, -jnp.inf)
        m_old = m_ref[...]
        m_new = jnp.maximum(m_old, jnp.max(s, axis=-1, keepdims=True))
        alpha = jnp.exp2(m_old - m_new)
        p = jnp.exp2(s - m_new).astype(BF16)
        v = load_v(j)
        acc_ref[...] = alpha * acc_ref[...] + _dot(p, jnp.concatenate([v, jnp.ones_like(v)], axis=1))
        m_ref[...] = m_new

    n_full = (qi * tq) // tk
    n_pairs = n_full // 2

    scores(0, sa_ref)

    def body(i, carry):
        scores(2 * i + 1, sb_ref)
        update(2 * i, sa_ref, False)
        scores(2 * i + 2, sa_ref)
        update(2 * i + 1, sb_ref, False)
        return carry

    lax.fori_loop(0, n_pairs, body, 0)
    e0 = 2 * n_pairs

    if tq == 2 * tk:
        scores(e0 + 1, sb_ref)
        update(e0, sa_ref, True)
        update(e0 + 1, sb_ref, True)
    else:
        tail2 = n_full - e0

        @pl.when(tail2 == 1)
        def _():
            scores(e0 + 1, sb_ref)
            update(e0, sa_ref, False)
            update(e0 + 1, sb_ref, True)

        @pl.when(tail2 == 0)
        def _():
            update(e0, sa_ref, True)

    return acc_ref[:, :LANES], acc_ref[:, LANES:LANES + 1]


def _flash_scratch(rows):
    return [pltpu.VMEM((rows, ATT_TK), F32), pltpu.VMEM((rows, ATT_TK), F32),
            pltpu.VMEM((rows, 1), F32), pltpu.VMEM((rows, 2 * LANES), F32)]


def _kv_block(j):
    return pl.ds(pl.multiple_of(j * ATT_TK, ATT_TK), ATT_TK)


def _da_attn_kernel(q_ref, k_ref, v_ref, lam_ref, sg_ref, o_ref, *scratch, lambda_init):
    qi = pl.program_id(2)
    tq = DA_TQ
    q = q_ref[...]
    lane = lax.broadcasted_iota(jnp.int32, q.shape, 1)
    zero = jnp.zeros_like(q)
    qs = jnp.concatenate([jnp.where(lane < DA_HEAD_DIM, q, zero),
                          jnp.where(lane >= DA_HEAD_DIM, q, zero)], axis=0)

    acc, l = _flash(qs, lambda j: k_ref[_kv_block(j), :], lambda j: v_ref[_kv_block(j), :],
                    qi, tq, 2, *scratch)
    o1 = acc[:tq] / l[:tq]
    o2 = acc[tq:] / l[tq:]
    lv = lam_ref[...]
    lam = (jnp.exp(jnp.sum(lv[0:1] * lv[1:2], axis=-1, keepdims=True))
           - jnp.exp(jnp.sum(lv[2:3] * lv[3:4], axis=-1, keepdims=True)) + lambda_init)
    o = o1 - lam * o2
    y = o * _rms(o, DA_V_DIM) * sg_ref[...] * (1.0 - lambda_init)
    o_ref[...] = y.astype(BF16)


def _da_attn(q, k, v, lam_vecs, subln_g, lambda_init):
    B, S, _ = q.shape
    tq = DA_TQ
    qspec = pl.BlockSpec((None, tq, LANES), lambda b, h, i: (b, i, h))
    kvspec = pl.BlockSpec((None, S, LANES), lambda b, h, i: (b, 0, h))
    return pl.pallas_call(
        functools.partial(_da_attn_kernel, lambda_init=lambda_init),
        out_shape=jax.ShapeDtypeStruct((B, S, D_MODEL), BF16),
        grid=(B, DA_HEADS, S // tq),
        in_specs=[qspec, kvspec, kvspec, _const_spec(lam_vecs.shape), _const_spec(subln_g.shape)],
        out_specs=qspec,
        scratch_shapes=_flash_scratch(2 * tq),
        compiler_params=pltpu.CompilerParams(
            dimension_semantics=("parallel", "parallel", "arbitrary"), vmem_limit_bytes=VMEM_LIMIT),
        name="da_attn")(q, k, v, lam_vecs, subln_g)


def _mla_attn_kernel(qn_ref, qr_ref, kn_ref, kr_ref, v_ref, o_ref, *scratch):
    qi = pl.program_id(2)
    qs = jnp.concatenate([qn_ref[...], qr_ref[...]], axis=-1)

    def load_k(j):
        return jnp.concatenate([kn_ref[_kv_block(j), :], kr_ref[_kv_block(j), :]], axis=-1)

    acc, l = _flash(qs, load_k, lambda j: v_ref[_kv_block(j), :], qi, MLA_TQ, 1, *scratch)
    o_ref[...] = (acc / l).astype(BF16)


def _mla_attn(qn, qr, kn, kr, vm):
    B, S, _ = qn.shape
    tq = MLA_TQ
    qspec = pl.BlockSpec((None, tq, LANES), lambda b, h, i: (b, i, h))
    kvspec = pl.BlockSpec((None, S, LANES), lambda b, h, i: (b, 0, h))
    krspec = pl.BlockSpec((None, S, LANES), lambda b, h, i: (b, 0, 0))
    return pl.pallas_call(
        _mla_attn_kernel,
        out_shape=jax.ShapeDtypeStruct((B, S, D_MODEL), BF16),
        grid=(B, MLA_HEADS, S // tq),
        in_specs=[qspec, qspec, kvspec, krspec, kvspec],
        out_specs=qspec,
        scratch_shapes=_flash_scratch(tq),
        compiler_params=pltpu.CompilerParams(
            dimension_semantics=("parallel", "parallel", "arbitrary"), vmem_limit_bytes=VMEM_LIMIT),
        name="mla_attn")(qn, qr, kn, kr, vm)


def _merge_kernel(oda_ref, omla_ref, gate_ref, x_ref, wo_ref, fg_ref, whi_ref, wlo_ref, br_ref,
                  x1_ref, h2_ref, ri_ref, rw_ref, cnt_ref, carry_ref):
    i = pl.program_id(0)
    rows = x_ref.shape[0]

    @pl.when(i == 0)
    def _():
        carry_ref[...] = jnp.zeros_like(carry_ref)

    g = gate_ref[...].astype(F32)
    mixed = g[:, :D_MODEL] * oda_ref[...].astype(F32) + g[:, D_MODEL:] * omla_ref[...].astype(F32)
    x1 = x_ref[...] + _dot(mixed.astype(BF16), wo_ref[...])
    x1_ref[...] = x1
    h2 = x1 * _rms(x1, D_MODEL) * fg_ref[...]
    for s in range(ROW_CHUNKS):
        h2_ref[pl.ds(s, rows, stride=ROW_CHUNKS), :] = h2[:, s * LANES:(s + 1) * LANES]

    hi = h2.astype(BF16)
    lo = (h2 - hi.astype(F32)).astype(BF16)
    whi = whi_ref[...]
    logits = _dot(hi, whi) + _dot(hi, wlo_ref[...]) + _dot(lo, whi) + br_ref[...]

    lane = lax.broadcasted_iota(jnp.int32, (rows, LANES), 1)
    neg = -jnp.inf
    gl = jnp.where(lane < N_GROUPS, logits, neg)
    gmax = jnp.max(gl, axis=-1, keepdims=True)
    g_sel = jnp.min(jnp.where(gl == gmax, lane, LANES), axis=-1, keepdims=True)
    p_g = 1.0 / jnp.sum(jnp.exp(gl - gmax), axis=-1, keepdims=True)
    e_lo = N_GROUPS + g_sel * EXPERTS_PER_GROUP
    el = jnp.where((lane >= e_lo) & (lane < e_lo + EXPERTS_PER_GROUP), logits, neg)
    v0 = jnp.max(el, axis=-1, keepdims=True)
    i0 = jnp.min(jnp.where(el == v0, lane, LANES), axis=-1, keepdims=True)
    el2 = jnp.where(lane == i0, neg, el)
    v1 = jnp.max(el2, axis=-1, keepdims=True)
    i1 = jnp.min(jnp.where(el2 == v1, lane, LANES), axis=-1, keepdims=True)
    t = jnp.exp(v1 - v0)
    w0 = p_g / (1.0 + t)
    w1 = p_g * t / (1.0 + t)
    e0 = i0 - N_GROUPS
    e1 = i1 - N_GROUPS

    oh = jnp.where((lane == e0) | (lane == e1), 1.0, 0.0)
    r_i = lax.broadcasted_iota(jnp.int32, (rows, rows), 0)
    c_i = lax.broadcasted_iota(jnp.int32, (rows, rows), 1)
    tri = jnp.where(c_i < r_i, 1.0, 0.0).astype(BF16)
    before = _dot(tri, oh.astype(BF16)) + carry_ref[...]
    rank0 = jnp.sum(jnp.where(lane == e0, before, 0.0), axis=-1, keepdims=True).astype(jnp.int32)
    rank1 = jnp.sum(jnp.where(lane == e1, before, 0.0), axis=-1, keepdims=True).astype(jnp.int32)
    carry = carry_ref[...] + jnp.sum(oh, axis=0, keepdims=True)
    carry_ref[...] = carry
    cnt_ref[...] = carry

    ri = jnp.where(lane == 0, e0, jnp.where(lane == 1, e1, jnp.where(lane == 2, rank0, rank1)))
    ri_ref[...] = ri[:, :SUBLANES]
    rw_ref[...] = jnp.where(lane == 0, w0, w1)[:, :SUBLANES]


def _merge(oda, omla, gates, x2, wo, fg, whi, wlo, br):
    T = x2.shape[0]
    tm = PROJ_ROWS
    row = lambda w: pl.BlockSpec((tm, w), lambda i: (i, 0))
    outs = [jax.ShapeDtypeStruct((T, D_MODEL), F32),
            jax.ShapeDtypeStruct((T * ROW_CHUNKS, LANES), F32),
            jax.ShapeDtypeStruct((T, SUBLANES), jnp.int32),
            jax.ShapeDtypeStruct((T, SUBLANES), F32),
            jax.ShapeDtypeStruct((1, LANES), F32)]
    out_specs = [row(D_MODEL), pl.BlockSpec((tm * ROW_CHUNKS, LANES), lambda i: (i, 0)),
                 row(SUBLANES), row(SUBLANES), pl.BlockSpec((1, LANES), lambda i: (0, 0))]
    return pl.pallas_call(
        _merge_kernel, out_shape=outs, grid=(T // tm,),
        in_specs=[row(D_MODEL), row(D_MODEL), row(2 * D_MODEL), row(D_MODEL), _const_spec(wo.shape),
                  _const_spec(fg.shape), _const_spec(whi.shape), _const_spec(wlo.shape),
                  _const_spec(br.shape)],
        out_specs=out_specs,
        scratch_shapes=[pltpu.VMEM((1, LANES), F32)],
        compiler_params=pltpu.CompilerParams(dimension_semantics=("arbitrary",),
                                             vmem_limit_bytes=VMEM_LIMIT),
        name="merge_router")(oda, omla, gates, x2, wo, fg, whi, wlo, br)


def _row(ref, r):
    return ref.at[pl.ds(pl.multiple_of(r * ROW_CHUNKS, ROW_CHUNKS), ROW_CHUNKS), :]


def _expert_kernel(ce_ref, nu_ref, tok_ref, tok_next_ref, h2_ref, w1_ref, w3_ref, w2_ref, yb_ref,
                   buf_ref, sem):
    c = pl.program_id(0)
    ch = MOE_CHUNK
    slot = c & 1
    n_used = nu_ref[0]

    def row_copy(toks, sl, t):
        return pltpu.make_async_copy(_row(h2_ref, toks[0, 0, t]), _row(buf_ref.at[sl], t),
                                     sem.at[sl])

    def slot_wait(sl):
        pltpu.make_async_copy(h2_ref.at[pl.ds(0, ch * ROW_CHUNKS), :], buf_ref.at[sl],
                              sem.at[sl]).wait()

    @pl.when(c == 0)
    def _():
        def issue(t, carry):
            row_copy(tok_ref, 0, t).start()
            return carry

        lax.fori_loop(0, ch, issue, 0, unroll=8)

    @pl.when(c < n_used)
    def _():
        slot_wait(slot)
        for t in range(ch):
            row_copy(tok_next_ref, 1 - slot, t).start(priority=t % 2)
        x = jnp.concatenate([buf_ref[slot, pl.ds(s, ch, stride=ROW_CHUNKS), :]
                             for s in range(ROW_CHUNKS)], axis=-1).astype(BF16)
        a = _dot(x, w1_ref[...])
        b = _dot(x, w3_ref[...])
        hmid = (a / (1.0 + jnp.exp(-a))) * b
        y = _dot(hmid.astype(BF16), w2_ref[...])
        for s in range(ROW_CHUNKS):
            yb_ref[pl.ds(s, ch, stride=ROW_CHUNKS), :] = y[:, s * LANES:(s + 1) * LANES]

    @pl.when(c >= n_used)
    def _():
        yb_ref[...] = jnp.zeros_like(yb_ref)

    @pl.when(c == n_used)
    def _():
        slot_wait(slot)


def _experts(chunk_e, n_used, tok3, h2r, w1, w3, w2):
    ch = MOE_CHUNK
    n_chunks = tok3.shape[0]
    smem = lambda f: pl.BlockSpec((1, 1, ch), f, memory_space=pltpu.SMEM)
    wspec = lambda s: pl.BlockSpec((None,) + s, lambda c, ce, nu: (ce[c], 0, 0))
    return pl.pallas_call(
        _expert_kernel,
        out_shape=jax.ShapeDtypeStruct((n_chunks * ch * ROW_CHUNKS, LANES), F32),
        grid_spec=pltpu.PrefetchScalarGridSpec(
            num_scalar_prefetch=2, grid=(n_chunks,),
            in_specs=[smem(lambda c, ce, nu: (c, 0, 0)),
                      smem(lambda c, ce, nu: (jnp.minimum(c + 1, n_chunks - 1), 0, 0)),
                      pl.BlockSpec(memory_space=pl.ANY),
                      wspec((D_MODEL, D_EXPERT)), wspec((D_MODEL, D_EXPERT)),
                      wspec((D_EXPERT, D_MODEL))],
            out_specs=pl.BlockSpec((ch * ROW_CHUNKS, LANES), lambda c, ce, nu: (c, 0)),
            scratch_shapes=[pltpu.VMEM((2, ch * ROW_CHUNKS, LANES), F32),
                            pltpu.SemaphoreType.DMA((2,))]),
        compiler_params=pltpu.CompilerParams(dimension_semantics=("arbitrary",),
                                             vmem_limit_bytes=VMEM_LIMIT),
        name="experts")(chunk_e, n_used, tok3, tok3, h2r, w1, w3, w2)


def _combine_kernel(dest_ref, dest_next_ref, yb_ref, x1_ref, rw_ref, o_ref, buf_ref, sem):
    i = pl.program_id(0)
    n = pl.num_programs(0)
    tb = ROUTE_ROWS
    slot = i & 1

    def gather(dests, sl):
        def issue(t, c):
            for k in range(TOP_K):
                pltpu.make_async_copy(_row(yb_ref, dests[0, 0, TOP_K * t + k]),
                                      _row(buf_ref.at[sl, k], t), sem.at[sl]).start(priority=k)
            return c

        lax.fori_loop(0, tb, issue, 0, unroll=4)

    @pl.when(i == 0)
    def _():
        gather(dest_ref, 0)

    @pl.when(i + 1 < n)
    def _():
        gather(dest_next_ref, 1 - slot)

    for k in range(TOP_K):
        pltpu.make_async_copy(yb_ref.at[pl.ds(0, tb * ROW_CHUNKS), :], buf_ref.at[slot, k],
                              sem.at[slot]).wait()
    rw = rw_ref[...]
    acc = x1_ref[...]
    for k in range(TOP_K):
        y = jnp.concatenate([buf_ref[slot, k, pl.ds(s, tb, stride=ROW_CHUNKS), :]
                             for s in range(ROW_CHUNKS)], axis=-1)
        acc = acc + y * rw[:, k:k + 1]
    o_ref[...] = acc


def _combine(dest3, yb, x1, rw):
    T = x1.shape[0]
    tb = ROUTE_ROWS
    nblk = T // tb
    smem = lambda f: pl.BlockSpec((1, 1, TOP_K * tb), f, memory_space=pltpu.SMEM)
    return pl.pallas_call(
        _combine_kernel,
        out_shape=jax.ShapeDtypeStruct((T, D_MODEL), F32),
        grid=(nblk,),
        in_specs=[smem(lambda i: (i, 0, 0)), smem(lambda i: (jnp.minimum(i + 1, nblk - 1), 0, 0)),
                  pl.BlockSpec(memory_space=pl.ANY),
                  pl.BlockSpec((tb, D_MODEL), lambda i: (i, 0)),
                  pl.BlockSpec((tb, SUBLANES), lambda i: (i, 0))],
        out_specs=pl.BlockSpec((tb, D_MODEL), lambda i: (i, 0)),
        scratch_shapes=[pltpu.VMEM((2, TOP_K, tb * ROW_CHUNKS, LANES), F32),
                        pltpu.SemaphoreType.DMA((2,))],
        compiler_params=pltpu.CompilerParams(dimension_semantics=("arbitrary",),
                                             vmem_limit_bytes=VMEM_LIMIT),
        name="combine")(dest3, dest3, yb, x1, rw)


def _rope_tables(positions, dim, half_lanes, one_fill):
    inv = ROPE_THETA ** (-jnp.arange(0, dim, 2, dtype=F32) / dim)
    ang = positions.astype(F32).reshape(-1, 1) * inv
    c, s = jnp.cos(ang), jnp.sin(ang)
    T = ang.shape[0]
    fill = half_lanes - dim
    cseg = jnp.concatenate([c, c, jnp.full((T, fill), one_fill, F32)], axis=-1)
    sseg = jnp.concatenate([-s, s, jnp.zeros((T, fill), F32)], axis=-1)
    reps = LANES // half_lanes
    return jnp.tile(cseg, (1, reps)), jnp.tile(sseg, (1, reps))


def _layer(x2, B, S, tables, lambda_init, attn_norm_g, w_in, b_gate, da_q_norm_g, da_k_norm_g,
           da_lambda_q1, da_lambda_k1, da_lambda_q2, da_lambda_k2, da_subln_g, mla_q_lora_g,
           mla_w_uq, mla_kv_lora_g, mla_w_ukv, mla_q_norm_g, mla_k_nope_norm_g, mla_k_rope_norm_g,
           w_o, ffn_norm_g, w_group, b_group, w_router, b_router, w1, w3, w2):
    T = B * S
    D = D_MODEL
    c0, c1, c2 = D, 2 * D, 3 * D
    c3 = c2 + MLA_Q_LORA
    c4 = c3 + MLA_KV_LORA
    c5 = c4 + MLA_ROPE
    wb = w_in.astype(BF16)
    wkr = jnp.pad(wb[:, c4:c5], ((0, 0), (0, LANES - MLA_ROPE)))
    qk = MLA_NOPE + MLA_ROPE
    uq = mla_w_uq.astype(BF16).reshape(MLA_Q_LORA, MLA_HEADS, qk)
    wuq = jnp.concatenate([
        uq[:, :, :MLA_NOPE].reshape(MLA_Q_LORA, D),
        jnp.pad(uq[:, :, MLA_NOPE:], ((0, 0), (0, 0), (0, LANES - MLA_ROPE))).reshape(MLA_Q_LORA, D)],
        axis=1)
    ukv = mla_w_ukv.astype(BF16).reshape(MLA_KV_LORA, MLA_HEADS, MLA_NOPE + MLA_V)
    wukv = jnp.concatenate([ukv[:, :, :MLA_NOPE].reshape(MLA_KV_LORA, D),
                            ukv[:, :, MLA_NOPE:].reshape(MLA_KV_LORA, D)], axis=1)
    wts = (wb[:, :c0], wb[:, c0:c1], wb[:, c1:c2], wb[:, c2:c3], wb[:, c3:c4], wkr, wb[:, c5:],
           wuq, wukv)
    r1 = lambda v: v.astype(F32).reshape(1, -1)
    pad_rope = lambda v: jnp.pad(v.astype(F32), (0, LANES - MLA_ROPE)).reshape(1, LANES)
    da_scale = DA_HEAD_DIM ** -0.5 * LOG2_E
    mla_scale = qk ** -0.5 * LOG2_E
    vecs = (r1(b_gate),
            jnp.tile(r1(da_q_norm_g), (1, 2)) * da_scale,
            jnp.tile(r1(da_k_norm_g), (1, 2)),
            r1(mla_q_lora_g), r1(mla_kv_lora_g),
            r1(mla_q_norm_g[:MLA_NOPE]) * mla_scale,
            pad_rope(mla_q_norm_g[MLA_NOPE:]) * mla_scale,
            r1(mla_k_nope_norm_g), pad_rope(mla_k_rope_norm_g))
    q, k, v, qn, qr, kn, vm, kr, gates = _in_proj(x2, r1(attn_norm_g), wts, vecs, tables)

    b3 = lambda a: a.reshape(B, S, a.shape[-1])
    lam_vecs = jnp.stack([da_lambda_q1, da_lambda_k1, da_lambda_q2, da_lambda_k2]).astype(F32)
    o_da = _da_attn(b3(q), b3(k), b3(v), lam_vecs, r1(da_subln_g), lambda_init)
    o_mla = _mla_attn(b3(qn), b3(qr), b3(kn), b3(kr), b3(vm))

    wgr = jnp.concatenate([w_group.astype(F32), w_router.astype(F32),
                           jnp.zeros((D, LANES - N_GROUPS - N_EXPERTS), F32)], axis=1)
    whi = wgr.astype(BF16)
    wlo = (wgr - whi.astype(F32)).astype(BF16)
    br = jnp.concatenate([b_group.astype(F32), b_router.astype(F32),
                          jnp.zeros((LANES - N_GROUPS - N_EXPERTS,), F32)]).reshape(1, LANES)
    x1, h2r, ri, rw, cnt = _merge(o_da.reshape(T, D), o_mla.reshape(T, D), gates, x2,
                                  w_o.astype(BF16), r1(ffn_norm_g), whi, wlo, br)

    ch = MOE_CHUNK
    counts = cnt[0, :N_EXPERTS].astype(jnp.int32)
    padded = (counts + ch - 1) // ch * ch
    pends = jnp.cumsum(padded)
    pstarts = pends - padded
    n_chunks = (T * TOP_K) // ch + N_EXPERTS
    P = n_chunks * ch
    e_ids = jnp.arange(N_EXPERTS, dtype=jnp.int32)
    seg_start = jnp.sum(jnp.where(ri[:, :TOP_K, None] == e_ids, pstarts, 0), axis=-1)
    dest = seg_start + ri[:, TOP_K:2 * TOP_K]
    dest3 = dest.reshape(T // ROUTE_ROWS, 1, TOP_K * ROUTE_ROWS)
    chunk_start = jnp.arange(n_chunks, dtype=jnp.int32)[:, None] * ch
    chunk_e = jnp.minimum(jnp.sum((pends[None, :] <= chunk_start).astype(jnp.int32), axis=1),
                          N_EXPERTS - 1).astype(jnp.int32)
    n_used = (pends[-1:] // ch).astype(jnp.int32)
    tok = jnp.repeat(jnp.arange(T, dtype=jnp.int32), TOP_K)
    tok3 = jnp.zeros((P,), jnp.int32).at[dest.reshape(-1)].set(
        tok, unique_indices=True, mode='promise_in_bounds').reshape(n_chunks, 1, ch)

    yb = _experts(chunk_e, n_used, tok3, h2r, w1.astype(BF16), w3.astype(BF16), w2.astype(BF16))
    return _combine(dest3, yb, x1, rw)


def kernel(x, positions, attn_norm_g, w_in, b_gate, da_q_norm_g, da_k_norm_g, da_lambda_q1, da_lambda_k1, da_lambda_q2, da_lambda_k2, da_subln_g, mla_q_lora_g, mla_w_uq, mla_kv_lora_g, mla_w_ukv, mla_q_norm_g, mla_k_nope_norm_g, mla_k_rope_norm_g, w_o, ffn_norm_g, w_group, b_group, w_router, b_router, w1, w3, w2):
    B, S, D = x.shape
    assert D == D_MODEL and S % max(DA_TQ, MLA_TQ, ATT_TK) == 0 and (B * S) % INPROJ_ROWS == 0
    tables = (_rope_tables(positions, DA_ROT_DIM, DA_HEAD_DIM, 1.0)
              + _rope_tables(positions, MLA_ROPE, LANES, 1.0))
    x2 = x.reshape(B * S, D)
    per_layer = (attn_norm_g, w_in, b_gate, da_q_norm_g, da_k_norm_g, da_lambda_q1, da_lambda_k1,
                 da_lambda_q2, da_lambda_k2, da_subln_g, mla_q_lora_g, mla_w_uq, mla_kv_lora_g,
                 mla_w_ukv, mla_q_norm_g, mla_k_nope_norm_g, mla_k_rope_norm_g, w_o, ffn_norm_g,
                 w_group, b_group, w_router, b_router, w1, w3, w2)
    for l in range(w_in.shape[0]):
        lambda_init = 0.8 - 0.6 * math.exp(-0.3 * l)
        x2 = _layer(x2, B, S, tables, lambda_init, *[p[l] for p in per_layer])
    return x2.reshape(B, S, D)
```
